```python
import math, functools
import jax, jax.numpy as jnp
from jax import lax
import numpy as np

D_MODEL = 1024
BATCH = 8
SEQ = 2048
DEPTH = 2
DEC_BATCH = 32
DEC_SEQ = 4
PAST_LEN = 16384
PAGE_SIZE = 128

N_MIXERS = 4
GROUP_W = D_MODEL // N_MIXERS
HEAD_DIM = 64
POOL_WINDOWS = (2, 4, 8, 16)
POOL_GROUPS = len(POOL_WINDOWS)
POOL_CH = GROUP_W // POOL_GROUPS
POOL_BUF = max(POOL_WINDOWS) - 1
DIFF_HEADS = GROUP_W // HEAD_DIM
DIFF_V = HEAD_DIM
DIFF_QK = HEAD_DIM // 2
ROT_DIM = DIFF_QK // 4
ROPE_THETA = 500000.0
SGU_GROUPS = 4
SGU_CH = GROUP_W // SGU_GROUPS
CHUNK = 128
SB_HEADS = GROUP_W // HEAD_DIM
SB_DIM = HEAD_DIM
N_PROJ_PARTS = 9
PROJ_W = N_PROJ_PARTS * GROUP_W
D_FF = 4 * D_MODEL
PLE_DIM = 256
Q_BLOCK = 128
EPS = 1e-6

kernel_name = 'hybrid_pool_diff_sgu_stick_decoder_step'


def rmsnorm(x, g):
    xf = x.astype(jnp.float32)
    y = xf * lax.rsqrt(jnp.mean(jnp.square(xf), axis=-1, keepdims=True) + EPS)
    return (y * g.astype(jnp.float32)).astype(x.dtype)


def heads(t, *shape):
    return t.reshape(t.shape[:2] + shape)


def partial_rope(x, pos):
    half = ROT_DIM // 2
    inv = ROPE_THETA ** (-jnp.arange(0, ROT_DIM, 2, dtype=jnp.float32) / ROT_DIM)
    ang = pos.astype(jnp.float32)[:, None] * inv[None, :]
    cos = jnp.cos(ang)[None, :, None, None, :]
    sin = jnp.sin(ang)[None, :, None, None, :]
    xr = x[..., :ROT_DIM].astype(jnp.float32)
    x1, x2 = xr[..., :half], xr[..., half:]
    rot = jnp.concatenate([x1 * cos - x2 * sin, x2 * cos + x1 * sin], axis=-1)
    return jnp.concatenate([rot.astype(x.dtype), x[..., ROT_DIM:]], axis=-1)


def _scores(q, ks):
    return jnp.concatenate([jnp.einsum('bqhd,bkhd->bhqk', q, k, preferred_element_type=jnp.float32) for k in ks], axis=-1)


def _apply(w, vs):
    out, off = None, 0
    for v in vs:
        n = v.shape[1]
        part = jnp.einsum('bhqk,bkhd->bqhd', w[..., off:off + n].astype(v.dtype), v)
        out = part if out is None else out + part
        off += n
    return out


def diff_core(q, qpos, ks, vs, kpos, lam):
    mask = (kpos[None, :] <= qpos[:, None])[None, None]
    scale = DIFF_QK ** -0.5

    def probs(j):
        s = _scores(q[..., j, :], [k[..., j, :] for k in ks]) * scale
        return jax.nn.softmax(jnp.where(mask, s, -jnp.inf), axis=-1)

    return _apply(probs(0) - lam * probs(1), vs)


def stick_core(q, qpos, ks, vs, kpos):
    mask = (kpos[None, :] < qpos[:, None])[None, None]
    z = _scores(q, ks) * (SB_DIM ** -0.5)
    log_keep = jnp.where(mask, jax.nn.log_sigmoid(-z), 0.0)
    log_a = jax.nn.log_sigmoid(z) + lax.cumsum(log_keep, axis=3, reverse=True) - log_keep
    a = jnp.exp(jnp.where(mask, log_a, -jnp.inf))
    return _apply(a, vs)


def block_sweep(core, q, k, v):
    B, S = q.shape[:2]
    nb = S // Q_BLOCK
    pos = jnp.arange(S)
    qb = jnp.moveaxis(q.reshape((B, nb, Q_BLOCK) + q.shape[2:]), 1, 0)
    out = lax.map(lambda args: core(args[0], args[1], [k], [v], pos), (qb, pos.reshape(nb, Q_BLOCK)))
    return jnp.moveaxis(out, 0, 1).reshape((B, S) + out.shape[3:])


def pool_mixer(a, buf, pos0, w_pool, scale):
    B, T, C = a.shape
    L = buf.shape[1]
    rows = jnp.concatenate([buf, a], axis=1)
    cs = jnp.concatenate([jnp.zeros((B, 1, C), jnp.float32), jnp.cumsum(rows.astype(jnp.float32), axis=1)], axis=1)
    end = L + 1 + jnp.arange(T)
    pos = pos0 + jnp.arange(T)
    means = []
    for g, w in enumerate(POOL_WINDOWS):
        c = cs[..., g * POOL_CH:(g + 1) * POOL_CH]
        tot = c[:, end] - c[:, jnp.maximum(end - w, 0)]
        cnt = jnp.minimum(pos + 1, w).astype(jnp.float32)
        means.append(tot / cnt[None, :, None])
    pooled = (jnp.concatenate(means, axis=-1) - a.astype(jnp.float32)).astype(a.dtype)
    mixed = jnp.einsum('btgc,gcd->btgd', pooled.reshape(B, T, POOL_GROUPS, POOL_CH), w_pool).reshape(B, T, C)
    return mixed * scale, rows[:, -POOL_BUF:]


def sgu_mixer(su, sv, w_s, b, chunk_len):
    u = jax.nn.gelu(su)
    v = jax.nn.gelu(sv)
    B, T, C = v.shape
    ws = (w_s * jnp.tril(jnp.ones((CHUNK, CHUNK), w_s.dtype)))[:, :chunk_len, :chunk_len]
    bb = b[:, :chunk_len]
    vc = v.reshape(B, T // chunk_len, chunk_len, SGU_GROUPS, SGU_CH)
    mixed = jnp.einsum('gts,bnsgc->bntgc', ws, vc) + bb.T[None, None, :, :, None]
    return u * mixed.reshape(B, T, C), v


def diff_post(o, g, lam_scale):
    o = rmsnorm(o, g) * lam_scale
    return o.reshape(o.shape[:2] + (GROUP_W,))


def project(x, w_in, g):
    return jnp.split(rmsnorm(x, g) @ w_in, N_PROJ_PARTS, axis=-1)


def layer_tail(x, mix, p, w_out, g_mix_post, g_ffn_pre, g_ffn_post, g_ple, w_up, w_down, w_gate, w_pp):
    x = x + rmsnorm(mix @ w_out, g_mix_post)
    h = rmsnorm(x, g_ffn_pre)
    f = jnp.square(jax.nn.relu(h @ w_up)) @ w_down
    x = x + rmsnorm(f, g_ffn_post)
    gate = jax.nn.sigmoid(x @ w_gate)
    return x + rmsnorm(gate * (p @ w_pp), g_ple)


def setup_inputs(seed: int = 0) -> dict:
    key = jax.random.key(seed)
    ks = iter(jax.random.split(key, 40))

    def nrm(shape, s):
        return jax.random.normal(next(ks), shape, jnp.float32) * s

    def gain(shape):
        return 1.0 + nrm(shape, 0.05)

    n_pages = PAST_LEN // PAGE_SIZE
    n_pool = (DEC_BATCH * n_pages * 5) // 4
    page_table = jax.random.permutation(next(ks), n_pool)[:DEC_BATCH * n_pages].reshape(DEC_BATCH, n_pages).astype(jnp.int32)
    return {
        'x_prompt': nrm((BATCH, SEQ, D_MODEL), 1.0),
        'x_sample': nrm((DEC_BATCH, DEC_SEQ, D_MODEL), 1.0),
        'state_pool': nrm((DEPTH, DEC_BATCH, POOL_BUF, GROUP_W), 1.0),
        'cache_diff_k': nrm((DEPTH, n_pool, PAGE_SIZE, DIFF_HEADS, 2, DIFF_QK), 1.0),
        'cache_diff_v': nrm((DEPTH, n_pool, PAGE_SIZE, DIFF_HEADS, DIFF_V), 1.0),
        'cache_sb_k': nrm((DEPTH, n_pool, PAGE_SIZE, SB_HEADS, SB_DIM), 1.0),
        'cache_sb_v': nrm((DEPTH, n_pool, PAGE_SIZE, SB_HEADS, SB_DIM), 1.0),
        'page_table': page_table,
        'p_prompt': nrm((DEPTH, BATCH, SEQ, PLE_DIM), 1.0),
        'p_sample': nrm((DEPTH, DEC_BATCH, DEC_SEQ, PLE_DIM), 1.0),
        'w_in': nrm((DEPTH, D_MODEL, PROJ_W), D_MODEL ** -0.5),
        'w_out': nrm((DEPTH, N_MIXERS * GROUP_W, D_MODEL), (N_MIXERS * GROUP_W) ** -0.5),
        'norm_mix_pre': gain((DEPTH, D_MODEL)),
        'norm_mix_post': gain((DEPTH, D_MODEL)),
        'norm_ffn_pre': gain((DEPTH, D_MODEL)),
        'norm_ffn_post': gain((DEPTH, D_MODEL)),
        'norm_ple': gain((DEPTH, D_MODEL)),
        'pool_w': nrm((DEPTH, POOL_GROUPS, POOL_CH, POOL_CH), POOL_CH ** -0.5),
        'pool_scale': gain((DEPTH, GROUP_W)),
        'diff_lam_q1': nrm((DEPTH, DIFF_QK), 0.1),
        'diff_lam_k1': nrm((DEPTH, DIFF_QK), 0.1),
        'diff_lam_q2': nrm((DEPTH, DIFF_QK), 0.1),
        'diff_lam_k2': nrm((DEPTH, DIFF_QK), 0.1),
        'diff_subln': gain((DEPTH, DIFF_V)),
        'sgu_w': nrm((DEPTH, SGU_GROUPS, CHUNK, CHUNK), CHUNK ** -0.5),
        'sgu_b': gain((DEPTH, SGU_GROUPS, CHUNK)),
        'w_ffn_up': nrm((DEPTH, D_MODEL, D_FF), D_MODEL ** -0.5),
        'w_ffn_down': nrm((DEPTH, D_FF, D_MODEL), D_FF ** -0.5),
        'w_ple_gate': nrm((DEPTH, D_MODEL, D_MODEL), D_MODEL ** -0.5),
        'w_ple_proj': nrm((DEPTH, PLE_DIM, D_MODEL), PLE_DIM ** -0.5),
    }


def reference(x_prompt, x_sample, state_pool, cache_diff_k, cache_diff_v, cache_sb_k, cache_sb_v, page_table,
              p_prompt, p_sample, w_in, w_out, norm_mix_pre, norm_mix_post, norm_ffn_pre, norm_ffn_post, norm_ple,
              pool_w, pool_scale, diff_lam_q1, diff_lam_k1, diff_lam_q2, diff_lam_k2, diff_subln,
              sgu_w, sgu_b, w_ffn_up, w_ffn_down, w_ple_gate, w_ple_proj):
    n_pages = PAST_LEN // PAGE_SIZE
    db, dec_seq = x_sample.shape[0], x_sample.shape[1]
    pos_p = jnp.arange(x_prompt.shape[1])
    pos_s = PAST_LEN + jnp.arange(dec_seq)
    kpos_s = jnp.concatenate([jnp.arange(PAST_LEN), pos_s])
    xp, xs = x_prompt, x_sample
    pool_p_l, pool_s_l = [], []
    dk_p_l, dv_p_l, sk_p_l, sv_p_l = [], [], [], []
    dk_s_l, dv_s_l, sk_s_l, sv_s_l = [], [], [], []
    sgu_s_l = []
    for i in range(DEPTH):
        lam_init = 0.8 - 0.6 * math.exp(-0.3 * i)
        lam = (jnp.exp(jnp.sum(diff_lam_q1[i].astype(jnp.float32) * diff_lam_k1[i].astype(jnp.float32)))
               - jnp.exp(jnp.sum(diff_lam_q2[i].astype(jnp.float32) * diff_lam_k2[i].astype(jnp.float32))) + lam_init)
        tail = functools.partial(layer_tail, w_out=w_out[i], g_mix_post=norm_mix_post[i], g_ffn_pre=norm_ffn_pre[i],
                                 g_ffn_post=norm_ffn_post[i], g_ple=norm_ple[i], w_up=w_ffn_up[i], w_down=w_ffn_down[i],
                                 w_gate=w_ple_gate[i], w_pp=w_ple_proj[i])

        a, dq, dk, dv, su, sv, sq, sk, svv = project(xp, w_in[i], norm_mix_pre[i])
        y_pool, pool_p = pool_mixer(a, a[:, :0], 0, pool_w[i], pool_scale[i])
        q = partial_rope(heads(dq, DIFF_HEADS, 2, DIFF_QK), pos_p)
        k = partial_rope(heads(dk, DIFF_HEADS, 2, DIFF_QK), pos_p)
        v = heads(dv, DIFF_HEADS, DIFF_V)
        y_diff = diff_post(block_sweep(functools.partial(diff_core, lam=lam), q, k, v), diff_subln[i], 1.0 - lam_init)
        y_sgu, _ = sgu_mixer(su, sv, sgu_w[i], sgu_b[i], CHUNK)
        bq, bk, bv = heads(sq, SB_HEADS, SB_DIM), heads(sk, SB_HEADS, SB_DIM), heads(svv, SB_HEADS, SB_DIM)
        y_sb = block_sweep(stick_core, bq, bk, bv)
        y_sb = y_sb.reshape(y_sb.shape[:2] + (GROUP_W,))
        xp = tail(xp, jnp.concatenate([y_pool, y_diff, y_sgu, y_sb], axis=-1), p_prompt[i])
        pool_p_l.append(pool_p); dk_p_l.append(k); dv_p_l.append(v); sk_p_l.append(bk); sv_p_l.append(bv)

        a, dq, dk, dv, su, sv, sq, sk, svv = project(xs, w_in[i], norm_mix_pre[i])
        y_pool, pool_s = pool_mixer(a, state_pool[i], PAST_LEN, pool_w[i], pool_scale[i])
        q = partial_rope(heads(dq, DIFF_HEADS, 2, DIFF_QK), pos_s)
        k = partial_rope(heads(dk, DIFF_HEADS, 2, DIFF_QK), pos_s)
        v = heads(dv, DIFF_HEADS, DIFF_V)
        k_past = cache_diff_k[i, page_table].reshape((db, n_pages * PAGE_SIZE) + k.shape[2:])
        v_past = cache_diff_v[i, page_table].reshape((db, n_pages * PAGE_SIZE) + v.shape[2:])
        y_diff = diff_post(diff_core(q, pos_s, [k_past, k], [v_past, v], kpos_s, lam), diff_subln[i], 1.0 - lam_init)
        y_sgu, sgu_v = sgu_mixer(su, sv, sgu_w[i], sgu_b[i], dec_seq)
        bq, bk, bv = heads(sq, SB_HEADS, SB_DIM), heads(sk, SB_HEADS, SB_DIM), heads(svv, SB_HEADS, SB_DIM)
        bk_past = cache_sb_k[i, page_table].reshape((db, n_pages * PAGE_SIZE) + bk.shape[2:])
        bv_past = cache_sb_v[i, page_table].reshape((db, n_pages * PAGE_SIZE) + bv.shape[2:])
        y_sb = stick_core(bq, pos_s, [bk_past, bk], [bv_past, bv], kpos_s)
        y_sb = y_sb.reshape(y_sb.shape[:2] + (GROUP_W,))
        xs = tail(xs, jnp.concatenate([y_pool, y_diff, y_sgu, y_sb], axis=-1), p_sample[i])
        pool_s_l.append(pool_s); dk_s_l.append(k); dv_s_l.append(v); sk_s_l.append(bk); sv_s_l.append(bv)
        sgu_s_l.append(sgu_v)

    return (xp, xs,
            jnp.stack(pool_p_l), jnp.stack(pool_s_l),
            jnp.stack(dk_p_l), jnp.stack(dv_p_l), jnp.stack(sk_p_l), jnp.stack(sv_p_l),
            jnp.stack(dk_s_l), jnp.stack(dv_s_l), jnp.stack(sk_s_l), jnp.stack(sv_s_l),
            jnp.stack(sgu_s_l))
```

```python
import functools
import math

import jax
import jax.numpy as jnp
from jax import lax
from jax.experimental import pallas as pl
from jax.experimental.pallas import tpu as pltpu

F32 = jnp.float32
BF16 = jnp.bfloat16

EPS = 1e-6
GROUP_W = 256
N_PARTS = 9
HEAD_DIM = 64
N_HEADS = GROUP_W // HEAD_DIM
DIFF_QK = HEAD_DIM // 2
ROT_DIM = DIFF_QK // 4
ROT_HALF = ROT_DIM // 2
ROPE_THETA = 500000.0
POOL_WINDOWS = (2, 4, 8, 16)
POOL_CH = GROUP_W // len(POOL_WINDOWS)
POOL_BUF = max(POOL_WINDOWS) - 1
HALO = POOL_BUF + 1
CHUNK = 128
PAGE = 128
SUB = 8
NEG = -1e30
VMEM_LIMIT = 56 * 1024 * 1024

_TRANS_B = (((1,), (1,)), ((), ()))


def _rms(x, g):
    return x * lax.rsqrt(jnp.mean(x * x, axis=-1, keepdims=True) + EPS) * g


def _softplus(z):
    return jnp.maximum(z, 0.0) + jnp.log(1.0 + jnp.exp(-jnp.abs(z)))


def _lam(lq1, lk1, lq2, lk2, lam_init):
    return (jnp.exp(jnp.sum(lq1[...] * lk1[...], axis=-1, keepdims=True))
            - jnp.exp(jnp.sum(lq2[...] * lk2[...], axis=-1, keepdims=True)) + lam_init)


def _split_bf16(x):
    hi = x.astype(BF16)
    lo = (x - hi.astype(F32)).astype(BF16)
    return hi, lo


def _proj_kernel(x_ref, g_ref, w_ref, cos_ref, sa_ref, sb_ref, *out_refs):
    xn = _rms(x_ref[...], g_ref[...]).astype(BF16)
    for part in range(N_PARTS):
        y = jnp.dot(xn, w_ref[:, part * GROUP_W:(part + 1) * GROUP_W], preferred_element_type=F32)
        if part in (1, 2):
            y = (y * cos_ref[...] + pltpu.roll(y, GROUP_W - ROT_HALF, 1) * sa_ref[...]
                 + pltpu.roll(y, ROT_HALF, 1) * sb_ref[...])
        out_refs[part][...] = y


def _proj(x2d, g, w_bf16, tables, tm):
    T, D = x2d.shape
    cos_t, sa_t, sb_t = tables
    pos_tiles = cos_t.shape[0] // tm
    tab_spec = pl.BlockSpec((tm, GROUP_W), lambda i: (i % pos_tiles, 0))
    return pl.pallas_call(
        _proj_kernel,
        grid=(T // tm,),
        in_specs=[pl.BlockSpec((tm, D), lambda i: (i, 0)),
                  pl.BlockSpec((1, D), lambda i: (0, 0)),
                  pl.BlockSpec((D, N_PARTS * GROUP_W), lambda i: (0, 0)),
                  tab_spec, tab_spec, tab_spec],
        out_specs=[pl.BlockSpec((tm, GROUP_W), lambda i: (i, 0))] * N_PARTS,
        out_shape=[jax.ShapeDtypeStruct((T, GROUP_W), F32)] * N_PARTS,
        compiler_params=pltpu.CompilerParams(dimension_semantics=("arbitrary",), vmem_limit_bytes=VMEM_LIMIT),
        name="proj",
    )(x2d, g, w_bf16, cos_t, sa_t, sb_t)


def _rope_tables(pos):
    inv = ROPE_THETA ** (-jnp.arange(0, ROT_DIM, 2, dtype=F32) / ROT_DIM)
    ang = pos.astype(F32)[:, None] * inv[None, :]
    cos, sin = jnp.cos(ang), jnp.sin(ang)
    n = pos.shape[0]
    pad = jnp.zeros((n, DIFF_QK - ROT_DIM), F32)
    zer = jnp.zeros((n, ROT_HALF), F32)
    reps = GROUP_W // DIFF_QK
    cos_t = jnp.tile(jnp.concatenate([cos, cos, pad + 1.0], axis=1), (1, reps))
    sa_t = jnp.tile(jnp.concatenate([-sin, zer, pad], axis=1), (1, reps))
    sb_t = jnp.tile(jnp.concatenate([zer, sin, pad], axis=1), (1, reps))
    return cos_t, sa_t, sb_t


def _pool_body(a, ext_ref, pos0, pw_ref, ps_ref):
    n = a.shape[0]
    lane_group = lax.broadcasted_iota(jnp.int32, (n, GROUP_W), 1) // POOL_CH
    pos1 = (pos0 + 1 + lax.broadcasted_iota(jnp.int32, (n, GROUP_W), 0)).astype(F32)
    run = a
    mean = None
    for sh in range(1, max(POOL_WINDOWS)):
        run = run + ext_ref[pl.ds(HALO - sh, n), :]
        if sh + 1 in POOL_WINDOWS:
            g = POOL_WINDOWS.index(sh + 1)
            m = run / jnp.minimum(pos1, float(sh + 1))
            mean = m if mean is None else jnp.where(lane_group == g, m, mean)
    pooled = (mean - a).astype(BF16)
    return jnp.dot(pooled, pw_ref[...], preferred_element_type=F32) * ps_ref[...]


def _pool_prompt_kernel(a_ref, pw_ref, ps_ref, o_ref, ext_ref, *, tm, tiles_per_seq):
    t = pl.program_id(0) % tiles_per_seq

    @pl.when(t == 0)
    def _():
        ext_ref[0:HALO, :] = jnp.zeros((HALO, GROUP_W), F32)

    @pl.when(t != 0)
    def _():
        ext_ref[0:HALO, :] = ext_ref[tm:tm + HALO, :]

    a = a_ref[...]
    ext_ref[HALO:HALO + tm, :] = a
    o_ref[...] = _pool_body(a, ext_ref, t * tm, pw_ref, ps_ref)


def _pool_sample_kernel(a_ref, halo_ref, pw_ref, ps_ref, o_ref, ext_ref, *, pos0):
    a = a_ref[0]
    ext_ref[0:HALO, :] = halo_ref[0]
    ext_ref[HALO:HALO + SUB, :] = a
    o_ref[0] = _pool_body(a, ext_ref, pos0, pw_ref, ps_ref)


def _pool_prompt(a2d, pw_bd, pscale, seq, tm):
    T = a2d.shape[0]
    return pl.pallas_call(
        functools.partial(_pool_prompt_kernel, tm=tm, tiles_per_seq=seq // tm),
        grid=(T // tm,),
        in_specs=[pl.BlockSpec((tm, GROUP_W), lambda i: (i, 0)),
                  pl.BlockSpec((GROUP_W, GROUP_W), lambda i: (0, 0)),
                  pl.BlockSpec((1, GROUP_W), lambda i: (0, 0))],
        out_specs=pl.BlockSpec((tm, GROUP_W), lambda i: (i, 0)),
        out_shape=jax.ShapeDtypeStruct((T, GROUP_W), F32),
        scratch_shapes=[pltpu.VMEM((tm + HALO, GROUP_W), F32)],
        compiler_params=pltpu.CompilerParams(dimension_semantics=("arbitrary",)),
        name="pool_prompt",
    )(a2d, pw_bd, pscale)


def _pool_sample(a3d, halo3d, pw_bd, pscale, pos0):
    nb = a3d.shape[0]
    return pl.pallas_call(
        functools.partial(_pool_sample_kernel, pos0=pos0),
        grid=(nb,),
        in_specs=[pl.BlockSpec((1, SUB, GROUP_W), lambda i: (i, 0, 0)),
                  pl.BlockSpec((1, HALO, GROUP_W), lambda i: (i, 0, 0)),
                  pl.BlockSpec((GROUP_W, GROUP_W), lambda i: (0, 0)),
                  pl.BlockSpec((1, GROUP_W), lambda i: (0, 0))],
        out_specs=pl.BlockSpec((1, SUB, GROUP_W), lambda i: (i, 0, 0)),
        out_shape=jax.ShapeDtypeStruct((nb, SUB, GROUP_W), F32),
        scratch_shapes=[pltpu.VMEM((HALO + SUB, GROUP_W), F32)],
        compiler_params=pltpu.CompilerParams(dimension_semantics=("arbitrary",)),
        name="pool_sample",
    )(a3d, halo3d, pw_bd, pscale)


def _sgu_kernel(su_ref, sv_ref, w_ref, b_ref, y_ref, v_ref, *, tm):
    row = lax.broadcasted_iota(jnp.int32, (CHUNK, CHUNK), 0)
    col = lax.broadcasted_iota(jnp.int32, (CHUNK, CHUNK), 1)
    tril = col <= row
    lane_group = lax.broadcasted_iota(jnp.int32, (CHUNK, GROUP_W), 1) // HEAD_DIM
    ws = [jnp.where(tril, w_ref[g], 0.0).astype(BF16) for g in range(N_HEADS)]
    for c in range(tm // CHUNK):
        rows = pl.ds(c * CHUNK, CHUNK)
        u = jax.nn.gelu(su_ref[rows, :])
        v = jax.nn.gelu(sv_ref[rows, :])
        v_ref[rows, :] = v
        vb = v.astype(BF16)
        mixed = None
        for g in range(N_HEADS):
            m = jnp.dot(ws[g], vb, preferred_element_type=F32)
            mixed = m if mixed is None else jnp.where(lane_group == g, m, mixed)
        y_ref[rows, :] = u * (mixed + b_ref[...])


def _sgu(su, sv, w, bias, tm):
    T = su.shape[0]
    row_spec = pl.BlockSpec((tm, GROUP_W), lambda i: (i, 0))
    return pl.pallas_call(
        functools.partial(_sgu_kernel, tm=tm),
        grid=(T // tm,),
        in_specs=[row_spec, row_spec,
                  pl.BlockSpec((N_HEADS, CHUNK, CHUNK), lambda i: (0, 0, 0)),
                  pl.BlockSpec((CHUNK, GROUP_W), lambda i: (0, 0))],
        out_specs=[row_spec, row_spec],
        out_shape=[jax.ShapeDtypeStruct((T, GROUP_W), F32)] * 2,
        compiler_params=pltpu.CompilerParams(dimension_semantics=("arbitrary",)),
        name="sgu",
    )(su, sv, w, bias)


def _diff_prompt_kernel(q_ref, k_ref, v_ref, lq1, lk1, lq2, lk2, g_ref, o_ref, m_ref, l_ref, acc_ref,
                        *, tq, lam_init):
    i = pl.program_id(1)
    lam = _lam(lq1, lk1, lq2, lk2, lam_init)
    scale = DIFF_QK ** -0.5
    lane = lax.broadcasted_iota(jnp.int32, (tq, HEAD_DIM), 1)
    row = lax.broadcasted_iota(jnp.int32, (tq, tq), 0)
    col = lax.broadcasted_iota(jnp.int32, (tq, tq), 1)
    causal = col <= row
    outs = []
    for h in range(N_HEADS):
        q = q_ref[0, h]
        qs = (jnp.where(lane < DIFF_QK, q, jnp.zeros_like(q)), jnp.where(lane >= DIFF_QK, q, jnp.zeros_like(q)))
        m_ref[...] = jnp.full(m_ref.shape, NEG, F32)
        l_ref[...] = jnp.zeros(l_ref.shape, F32)
        acc_ref[...] = jnp.zeros(acc_ref.shape, F32)

        def step(k, v, mask):
            for j in range(2):
                s = lax.dot_general(qs[j], k, _TRANS_B, preferred_element_type=F32) * scale
                if mask is not None:
                    s = jnp.where(mask, s, NEG)
                m_old = m_ref[j]
                m_new = jnp.maximum(m_old, jnp.max(s, axis=1, keepdims=True))
                alpha = jnp.exp(m_old - m_new)
                p = jnp.exp(s - m_new)
                l_ref[j] = alpha * l_ref[j] + jnp.sum(p, axis=1, keepdims=True)
                acc_ref[j] = alpha * acc_ref[j] + jnp.dot(p.astype(BF16), v, preferred_element_type=F32)
                m_ref[j] = m_new

        def body(jb, carry):
            off = pl.multiple_of(jb * tq, tq)
            step(k_ref[0, h, pl.ds(off, tq), :], v_ref[0, h, pl.ds(off, tq), :], None)
            return carry

        lax.fori_loop(0, i, body, 0)
        off = pl.multiple_of(i * tq, tq)
        step(k_ref[0, h, pl.ds(off, tq), :], v_ref[0, h, pl.ds(off, tq), :], causal)
        o = acc_ref[0] / l_ref[0] - lam * (acc_ref[1] / l_ref[1])
        outs.append(_rms(o, g_ref[...]) * (1.0 - lam_init))
    o_ref[0] = jnp.concatenate(outs, axis=-1)


def _sb_prompt_kernel(q_ref, k_ref, v_ref, o_ref, c_ref, acc_ref, *, tq):
    i = pl.program_id(1)
    scale = HEAD_DIM ** -0.5
    row = lax.broadcasted_iota(jnp.int32, (tq, tq), 0)
    col = lax.broadcasted_iota(jnp.int32, (tq, tq), 1)
    strict = col < row
    after = jnp.where(row > col, 1.0, 0.0).astype(BF16)
    outs = []
    for h in range(N_HEADS):
        q = q_ref[0, h]
        c_ref[...] = jnp.zeros(c_ref.shape, F32)
        acc_ref[...] = jnp.zeros(acc_ref.shape, F32)

        def step(k, v, mask):
            z = lax.dot_general(q, k, _TRANS_B, preferred_element_type=F32) * scale
            sp = _softplus(z)
            log_keep = -sp if mask is None else jnp.where(mask, -sp, 0.0)
            hi, lo = _split_bf16(log_keep)
            right = (jnp.dot(hi, after, preferred_element_type=F32)
                     + jnp.dot(lo, after, preferred_element_type=F32))
            a = jnp.exp((z - sp) + right + c_ref[...])
            if mask is not None:
                a = jnp.where(mask, a, 0.0)
            acc_ref[...] += jnp.dot(a.astype(BF16), v, preferred_element_type=F32)
            c_ref[...] += jnp.sum(log_keep, axis=1, keepdims=True)

        off = pl.multiple_of(i * tq, tq)
        step(k_ref[0, h, pl.ds(off, tq), :], v_ref[0, h, pl.ds(off, tq), :], strict)

        def body(it, carry):
            off = pl.multiple_of((i - 1 - it) * tq, tq)
            step(k_ref[0, h, pl.ds(off, tq), :], v_ref[0, h, pl.ds(off, tq), :], None)
            return carry

        lax.fori_loop(0, i, body, 0)
        outs.append(acc_ref[...])
    o_ref[0] = jnp.concatenate(outs, axis=-1)


def _head_major(t2d, batch, seq):
    return t2d.reshape(batch, seq, N_HEADS, HEAD_DIM).transpose(0, 2, 1, 3).astype(BF16)


def _attn_specs(seq, tq):
    q_spec = pl.BlockSpec((1, N_HEADS, tq, HEAD_DIM), lambda b, i: (b, 0, i, 0))
    kv_spec = pl.BlockSpec((1, N_HEADS, seq, HEAD_DIM), lambda b, i: (b, 0, 0, 0))
    o_spec = pl.BlockSpec((1, tq, GROUP_W), lambda b, i: (b, i, 0))
    return q_spec, kv_spec, o_spec


def _diff_prompt(q, k, v, lams, subln, lam_init, tq):
    batch, _, seq, _ = q.shape
    q_spec, kv_spec, o_spec = _attn_specs(seq, tq)
    vec = pl.BlockSpec((1, DIFF_QK), lambda b, i: (0, 0))
    return pl.pallas_call(
        functools.partial(_diff_prompt_kernel, tq=tq, lam_init=lam_init),
        grid=(batch, seq // tq),
        in_specs=[q_spec, kv_spec, kv_spec, vec, vec, vec, vec, pl.BlockSpec((1, HEAD_DIM), lambda b, i: (0, 0))],
        out_specs=o_spec,
        out_shape=jax.ShapeDtypeStruct((batch, seq, GROUP_W), F32),
        scratch_shapes=[pltpu.VMEM((2, tq, 1), F32), pltpu.VMEM((2, tq, 1), F32),
                        pltpu.VMEM((2, tq, HEAD_DIM), F32)],
        compiler_params=pltpu.CompilerParams(dimension_semantics=("arbitrary", "arbitrary")),
        name="diff_prompt",
    )(q, k, v, *lams, subln)


def _sb_prompt(q, k, v, tq):
    batch, _, seq, _ = q.shape
    q_spec, kv_spec, o_spec = _attn_specs(seq, tq)
    return pl.pallas_call(
        functools.partial(_sb_prompt_kernel, tq=tq),
        grid=(batch, seq // tq),
        in_specs=[q_spec, kv_spec, kv_spec],
        out_specs=o_spec,
        out_shape=jax.ShapeDtypeStruct((batch, seq, GROUP_W), F32),
        scratch_shapes=[pltpu.VMEM((tq, 1), F32), pltpu.VMEM((tq, HEAD_DIM), F32)],
        compiler_params=pltpu.CompilerParams(dimension_semantics=("arbitrary", "arbitrary")),
        name="sb_prompt",
    )(q, k, v)


def _decode_kernel(pt_ref, dq_ref, dk_ref, dv_ref, sq_ref, sk_ref, sv_ref, lq1, lk1, lq2, lk2, g_ref, *rest,
                   pages_per_step, n_new, lam_init):
    del pt_ref
    G = pages_per_step
    cdk = rest[0:G]
    cdv = rest[G:2 * G]
    csk = rest[2 * G:3 * G]
    csv = rest[3 * G:4 * G]
    yd_ref, ys_ref = rest[4 * G:4 * G + 2]
    qd_ref, qs_ref, new_ref, m_ref, l_ref, accd_ref, c_ref, accs_ref = rest[4 * G + 2:]
    p = pl.program_id(1)
    nd = 2 * N_HEADS * SUB
    ns = N_HEADS * SUB
    d_scale = DIFF_QK ** -0.5
    s_scale = HEAD_DIM ** -0.5
    row = lax.broadcasted_iota(jnp.int32, (PAGE, PAGE), 0)
    col = lax.broadcasted_iota(jnp.int32, (PAGE, PAGE), 1)
    after = jnp.where(row > col, 1.0, 0.0).astype(BF16)

    def diff_page(k, v, mask):
        s = lax.dot_general(qd_ref[...], k, _TRANS_B, preferred_element_type=F32) * d_scale
        if mask is not None:
            s = jnp.where(mask, s, NEG)
        m_old = m_ref[...]
        m_new = jnp.maximum(m_old, jnp.max(s, axis=1, keepdims=True))
        alpha = jnp.exp(m_old - m_new)
        pr = jnp.exp(s - m_new)
        if mask is not None:
            pr = jnp.where(mask, pr, 0.0)
        l_ref[...] = alpha * l_ref[...] + jnp.sum(pr, axis=1, keepdims=True)
        accd_ref[...] = alpha * accd_ref[...] + jnp.dot(pr.astype(BF16), v, preferred_element_type=F32)
        m_ref[...] = m_new

    def sb_page(k, v, mask):
        z = lax.dot_general(qs_ref[...], k, _TRANS_B, preferred_element_type=F32) * s_scale
        sp = _softplus(z)
        log_keep = -sp if mask is None else jnp.where(mask, -sp, 0.0)
        hi, lo = _split_bf16(log_keep)
        right = jnp.dot(hi, after, preferred_element_type=F32) + jnp.dot(lo, after, preferred_element_type=F32)
        a = jnp.exp((z - sp) + right + c_ref[...])
        if mask is not None:
            a = jnp.where(mask, a, 0.0)
        accs_ref[...] += jnp.dot(a.astype(BF16), v, preferred_element_type=F32)
        c_ref[...] += jnp.sum(log_keep, axis=1, keepdims=True)

    @pl.when(p == 0)
    def _():
        lane = lax.broadcasted_iota(jnp.int32, (SUB, GROUP_W), 1)
        dq = dq_ref[0]
        sq = sq_ref[0]
        for j in range(2):
            for h in range(N_HEADS):
                sel = (lane // DIFF_QK) == (2 * h + j)
                r0 = (j * N_HEADS + h) * SUB
                qd_ref[r0:r0 + SUB, :] = jnp.where(sel, dq, 0.0).astype(BF16)
        for h in range(N_HEADS):
            sel = (lane // HEAD_DIM) == h
            qs_ref[h * SUB:(h + 1) * SUB, :] = jnp.where(sel, sq, 0.0).astype(BF16)
        m_ref[...] = jnp.full(m_ref.shape, NEG, F32)
        l_ref[...] = jnp.zeros(l_ref.shape, F32)
        accd_ref[...] = jnp.zeros(accd_ref.shape, F32)
        c_ref[...] = jnp.zeros(c_ref.shape, F32)
        accs_ref[...] = jnp.zeros(accs_ref.shape, F32)
        zero_tail = jnp.zeros((PAGE - SUB, GROUP_W), BF16)
        tok_d = lax.broadcasted_iota(jnp.int32, (nd, PAGE), 0) % SUB
        key_d = lax.broadcasted_iota(jnp.int32, (nd, PAGE), 1)
        tok_s = lax.broadcasted_iota(jnp.int32, (ns, PAGE), 0) % SUB
        key_s = lax.broadcasted_iota(jnp.int32, (ns, PAGE), 1)

        def padded(ref):
            new_ref[0:SUB, :] = ref[0].astype(BF16)
            new_ref[SUB:PAGE, :] = zero_tail
            return new_ref[...]

        k_new = padded(dk_ref)
        sk_new = padded(sk_ref)
        v_new = padded(dv_ref)
        sv_new = padded(sv_ref)
        diff_page(k_new, v_new, (key_d <= tok_d) & (key_d < n_new))
        sb_page(sk_new, sv_new, (key_s < tok_s) & (key_s < n_new))

    for g in range(G):
        diff_page(cdk[g][...].astype(BF16), cdv[g][...].astype(BF16), None)
        sb_page(csk[g][...].astype(BF16), csv[g][...].astype(BF16), None)

    @pl.when(p == pl.num_programs(1) - 1)
    def _():
        lam = _lam(lq1, lk1, lq2, lk2, lam_init)
        lane_head = lax.broadcasted_iota(jnp.int32, (SUB, GROUP_W), 1) // HEAD_DIM
        yd = jnp.zeros((SUB, GROUP_W), F32)
        ys = jnp.zeros((SUB, GROUP_W), F32)
        for h in range(N_HEADS):
            r1 = h * SUB
            r2 = (N_HEADS + h) * SUB
            o = (accd_ref[r1:r1 + SUB, :] / l_ref[r1:r1 + SUB, :]
                 - lam * (accd_ref[r2:r2 + SUB, :] / l_ref[r2:r2 + SUB, :]))
            sel = lane_head == h
            ms = jnp.sum(jnp.where(sel, o * o, 0.0), axis=-1, keepdims=True) / HEAD_DIM
            yd = jnp.where(sel, o * lax.rsqrt(ms + EPS), yd)
            ys = jnp.where(sel, accs_ref[r1:r1 + SUB, :], ys)
        yd_ref[0] = yd * g_ref[...] * (1.0 - lam_init)
        ys_ref[0] = ys


def _decode(layer, page_table, new_rows, caches, lams, subln_tiled, lam_init, n_new, pages_per_step):
    nb, n_pages = page_table.shape
    G = pages_per_step
    steps = n_pages // G
    tok_spec = pl.BlockSpec((1, SUB, GROUP_W), lambda b, p, pt: (b, 0, 0))
    vec = pl.BlockSpec((1, DIFF_QK), lambda b, p, pt: (0, 0))

    def page_spec(g):
        return pl.BlockSpec((None, None, PAGE, GROUP_W),
                            lambda b, p, pt: (layer, pt[b, n_pages - 1 - (p * G + g)], 0, 0))

    page_specs = [page_spec(g) for g in range(G)]
    in_specs = ([tok_spec] * 6 + [vec] * 4 + [pl.BlockSpec((1, GROUP_W), lambda b, p, pt: (0, 0))]
                + page_specs * 4)
    cache_args = []
    for c in caches:
        cache_args += [c] * G
    nd = 2 * N_HEADS * SUB
    ns = N_HEADS * SUB
    return pl.pallas_call(
        functools.partial(_decode_kernel, pages_per_step=G, n_new=n_new, lam_init=lam_init),
        grid_spec=pltpu.PrefetchScalarGridSpec(
            num_scalar_prefetch=1,
            grid=(nb, steps),
            in_specs=in_specs,
            out_specs=[tok_spec, tok_spec],
            scratch_shapes=[pltpu.VMEM((nd, GROUP_W), BF16), pltpu.VMEM((ns, GROUP_W), BF16),
                            pltpu.VMEM((PAGE, GROUP_W), BF16),
                            pltpu.VMEM((nd, 1), F32), pltpu.VMEM((nd, 1), F32), pltpu.VMEM((nd, GROUP_W), F32),
                            pltpu.VMEM((ns, 1), F32), pltpu.VMEM((ns, GROUP_W), F32)]),
        out_shape=[jax.ShapeDtypeStruct((nb, SUB, GROUP_W), F32)] * 2,
        compiler_params=pltpu.CompilerParams(dimension_semantics=("arbitrary", "arbitrary")),
        name="decode",
    )(page_table, *new_rows, *lams, subln_tiled, *cache_args)


def _tail_kernel(x_ref, yp_ref, yd_ref, ys_ref, yb_ref, p_ref, wout_ref, wup_ref, wdown_ref, wgate_ref, wpp_ref,
                 g_mix, g_pre, g_post, g_ple, o_ref, *, ff_chunk):
    mix = jnp.concatenate([yp_ref[...], yd_ref[...], ys_ref[...], yb_ref[...]], axis=-1).astype(BF16)
    x = x_ref[...] + _rms(jnp.dot(mix, wout_ref[...], preferred_element_type=F32), g_mix[...])
    h = _rms(x, g_pre[...]).astype(BF16)
    f = None
    for c in range(wup_ref.shape[1] // ff_chunk):
        cols = slice(c * ff_chunk, (c + 1) * ff_chunk)
        u = jnp.dot(h, wup_ref[:, cols], preferred_element_type=F32)
        u = jnp.square(jnp.maximum(u, 0.0)).astype(BF16)
        d = jnp.dot(u, wdown_ref[cols, :], preferred_element_type=F32)
        f = d if f is None else f + d
    x = x + _rms(f, g_post[...])
    gate = jax.nn.sigmoid(jnp.dot(x.astype(BF16), wgate_ref[...], preferred_element_type=F32))
    pp = jnp.dot(p_ref[...].astype(BF16), wpp_ref[...], preferred_element_type=F32)
    o_ref[...] = x + _rms(gate * pp, g_ple[...])


def _tail(x2d, ys, p2d, weights, gains, tm):
    T, D = x2d.shape
    wout, wup, wdown, wgate, wpp = weights
    dff = wup.shape[1]

    def resident(shape):
        return pl.BlockSpec(shape, lambda i: (0,) * len(shape), pipeline_mode=pl.Buffered(1))

    row = lambda w: pl.BlockSpec((tm, w), lambda i: (i, 0))
    return pl.pallas_call(
        functools.partial(_tail_kernel, ff_chunk=min(dff, 1024)),
        grid=(T // tm,),
        in_specs=[row(D)] + [row(GROUP_W)] * 4 + [row(p2d.shape[1]),
                  resident(wout.shape), resident(wup.shape), resident(wdown.shape), resident(wgate.shape),
                  resident(wpp.shape)] + [resident((1, D))] * 4,
        out_specs=row(D),
        out_shape=jax.ShapeDtypeStruct((T, D), F32),
        compiler_params=pltpu.CompilerParams(dimension_semantics=("arbitrary",), vmem_limit_bytes=VMEM_LIMIT),
        name="tail",
    )(x2d, *ys, p2d, wout, wup, wdown, wgate, wpp, *gains)


def _block_diag(w):
    n, c, _ = w.shape
    eye = jnp.eye(n, dtype=w.dtype)
    return (eye[:, None, :, None] * w[:, :, None, :]).reshape(n * c, n * c)


def kernel(x_prompt, x_sample, state_pool, cache_diff_k, cache_diff_v, cache_sb_k, cache_sb_v, page_table,
           p_prompt, p_sample, w_in, w_out, norm_mix_pre, norm_mix_post, norm_ffn_pre, norm_ffn_post, norm_ple,
           pool_w, pool_scale, diff_lam_q1, diff_lam_k1, diff_lam_q2, diff_lam_k2, diff_subln,
           sgu_w, sgu_b, w_ffn_up, w_ffn_down, w_ple_gate, w_ple_proj):
    batch, seq, d_model = x_prompt.shape
    nb, dec_seq, _ = x_sample.shape
    depth = w_in.shape[0]
    n_pages = page_table.shape[1]
    past_len = n_pages * PAGE
    n_pool = cache_diff_k.shape[1]
    ple_dim = p_prompt.shape[-1]
    assert d_model == 4 * GROUP_W and dec_seq <= SUB and seq % CHUNK == 0 and (nb * SUB) % CHUNK == 0

    tm = min(512, seq)
    tq = min(256, seq)
    ts = nb * SUB
    pages_per_step = 4 if n_pages % 4 == 0 else 1

    tables_p = _rope_tables(jnp.arange(seq))
    tables_s = _rope_tables(jnp.tile(past_len + jnp.arange(SUB), nb))

    xp = x_prompt.reshape(batch * seq, d_model)
    xs = jnp.pad(x_sample, ((0, 0), (0, SUB - dec_seq), (0, 0))).reshape(ts, d_model)
    pad_s = lambda t: jnp.pad(t, ((0, 0), (0, SUB - dec_seq), (0, 0))).reshape(ts, -1)
    caches = [c.reshape(depth, n_pool, PAGE, GROUP_W) for c in (cache_diff_k, cache_diff_v, cache_sb_k, cache_sb_v)]

    outs = {name: [] for name in ("pool_p", "pool_s", "dk_p", "dv_p", "sk_p", "sv_p",
                                  "dk_s", "dv_s", "sk_s", "sv_s", "sgu_s")}
    unpad = lambda t: t.reshape(nb, SUB, GROUP_W)[:, :dec_seq]

    for i in range(depth):
        lam_init = 0.8 - 0.6 * math.exp(-0.3 * i)
        w_in_b = w_in[i].astype(BF16)
        weights = tuple(w[i].astype(BF16) for w in (w_out, w_ffn_up, w_ffn_down, w_ple_gate, w_ple_proj))
        gains = tuple(g[i].reshape(1, d_model) for g in (norm_mix_post, norm_ffn_pre, norm_ffn_post, norm_ple))
        g_pre = norm_mix_pre[i].reshape(1, d_model)
        pw_bd = _block_diag(pool_w[i]).astype(BF16)
        pscale = pool_scale[i].reshape(1, GROUP_W)
        lams = tuple(v[i].reshape(1, DIFF_QK) for v in (diff_lam_q1, diff_lam_k1, diff_lam_q2, diff_lam_k2))
        subln = diff_subln[i].reshape(1, HEAD_DIM)
        sgu_bias_p = jnp.repeat(sgu_b[i].T, HEAD_DIM, axis=1)
        reps = CHUNK // SUB
        sgu_w_s = jax.vmap(lambda w: jnp.kron(jnp.eye(reps, dtype=w.dtype), w[:SUB, :SUB]))(sgu_w[i])
        sgu_bias_s = jnp.tile(jnp.repeat(sgu_b[i][:, :SUB].T, HEAD_DIM, axis=1), (reps, 1))

        a, dq, dk, dv, su, sv, sq, sk, svv = _proj(xp, g_pre, w_in_b, tables_p, tm)
        y_pool = _pool_prompt(a, pw_bd, pscale, seq, tm)
        y_sgu, _ = _sgu(su, sv, sgu_w[i], sgu_bias_p, tm)
        hm = lambda t: _head_major(t, batch, seq)
        y_diff = _diff_prompt(hm(dq), hm(dk), hm(dv), lams, subln, lam_init, tq).reshape(batch * seq, GROUP_W)
        y_sb = _sb_prompt(hm(sq), hm(sk), hm(svv), tq).reshape(batch * seq, GROUP_W)
        xp = _tail(xp, (y_pool, y_diff, y_sgu, y_sb), p_prompt[i].reshape(batch * seq, ple_dim), weights, gains, tm)
        outs["pool_p"].append(a.reshape(batch, seq, GROUP_W)[:, seq - POOL_BUF:])
        outs["dk_p"].append(dk.reshape(batch, seq, N_HEADS, 2, DIFF_QK))
        outs["dv_p"].append(dv.reshape(batch, seq, N_HEADS, HEAD_DIM))
        outs["sk_p"].append(sk.reshape(batch, seq, N_HEADS, HEAD_DIM))
        outs["sv_p"].append(svv.reshape(batch, seq, N_HEADS, HEAD_DIM))

        a, dq, dk, dv, su, sv, sq, sk, svv = _proj(xs, g_pre, w_in_b, tables_s, ts)
        halo = jnp.concatenate([jnp.zeros((nb, HALO - POOL_BUF, GROUP_W), F32), state_pool[i]], axis=1)
        y_pool = _pool_sample(a.reshape(nb, SUB, GROUP_W), halo, pw_bd, pscale, past_len).reshape(ts, GROUP_W)
        y_sgu, sgu_v = _sgu(su, sv, sgu_w_s, sgu_bias_s, CHUNK)
        new_rows = tuple(t.reshape(nb, SUB, GROUP_W) for t in (dq, dk, dv, sq, sk, svv))
        y_diff, y_sb = _decode(i, page_table, new_rows, caches, lams, jnp.tile(subln, (1, N_HEADS)), lam_init,
                               dec_seq, pages_per_step)
        xs = _tail(xs, (y_pool, y_diff.reshape(ts, GROUP_W), y_sgu, y_sb.reshape(ts, GROUP_W)),
                   pad_s(p_sample[i]), weights, gains, ts)
        outs["pool_s"].append(jnp.concatenate([state_pool[i], unpad(a)], axis=1)[:, -POOL_BUF:])
        outs["dk_s"].append(unpad(dk).reshape(nb, dec_seq, N_HEADS, 2, DIFF_QK))
        outs["dv_s"].append(unpad(dv).reshape(nb, dec_seq, N_HEADS, HEAD_DIM))
        outs["sk_s"].append(unpad(sk).reshape(nb, dec_seq, N_HEADS, HEAD_DIM))
        outs["sv_s"].append(unpad(svv).reshape(nb, dec_seq, N_HEADS, HEAD_DIM))
        outs["sgu_s"].append(unpad(sgu_v))

    st = lambda name: jnp.stack(outs[name])
    return (xp.reshape(batch, seq, d_model), xs.reshape(nb, SUB, d_model)[:, :dec_seq],
            st("pool_p"), st("pool_s"),
            st("dk_p"), st("dv_p"), st("sk_p"), st("sv_p"),
            st("dk_s"), st("dv_s"), st("sk_s"), st("sv_s"),
            st("sgu_s"))
```

```python
import functools
import math

import jax
import jax.numpy as jnp
from jax import lax
from jax.experimental import pallas as pl
from jax.experimental.pallas import tpu as pltpu

F32 = jnp.float32
BF16 = jnp.bfloat16

EPS = 1e-6
GROUP_W = 256
N_PARTS = 9
HEAD_DIM = 64
N_HEADS = GROUP_W // HEAD_DIM
DIFF_QK = HEAD_DIM // 2
ROT_DIM = DIFF_QK // 4
ROT_HALF = ROT_DIM // 2
ROPE_THETA = 500000.0
POOL_WINDOWS = (2, 4, 8, 16)
POOL_CH = GROUP_W // len(POOL_WINDOWS)
POOL_BUF = max(POOL_WINDOWS) - 1
HALO = POOL_BUF + 1
CHUNK = 128
PAGE = 128
SUB = 8
NEG = -1e30
VMEM_LIMIT = 56 * 1024 * 1024

_TRANS_B = (((1,), (1,)), ((), ()))


def _rms(x, g):
    return x * lax.rsqrt(jnp.mean(x * x, axis=-1, keepdims=True) + EPS) * g


def _softplus(z):
    return jnp.maximum(z, 0.0) + jnp.log(1.0 + jnp.exp(-jnp.abs(z)))


def _lam(lq1, lk1, lq2, lk2, lam_init):
    return (jnp.exp(jnp.sum(lq1[...] * lk1[...], axis=-1, keepdims=True))
            - jnp.exp(jnp.sum(lq2[...] * lk2[...], axis=-1, keepdims=True)) + lam_init)


def _split_bf16(x):
    hi = x.astype(BF16)
    lo = (x - hi.astype(F32)).astype(BF16)
    return hi, lo


_T_ONLY = (1, 3, 6, 8)
_T_BOTH = (2, 7)


def _proj_kernel(x_ref, g_ref, w_ref, cos_ref, sa_ref, sb_ref, *refs, transposed):
    out_refs = refs[:-1] if transposed else refs
    xn = _rms(x_ref[...], g_ref[...]).astype(BF16)
    k = 0
    for part in range(N_PARTS):
        y = jnp.dot(xn, w_ref[:, part * GROUP_W:(part + 1) * GROUP_W], preferred_element_type=F32)
        if part in (1, 2):
            y = (y * cos_ref[...] + pltpu.roll(y, GROUP_W - ROT_HALF, 1) * sa_ref[...]
                 + pltpu.roll(y, ROT_HALF, 1) * sb_ref[...])
        staged = refs[-1]
        if not transposed or part not in _T_ONLY:
            staged = out_refs[k]
            k += 1
        staged[...] = y
        if transposed and part in _T_ONLY + _T_BOTH:
            out_refs[k][0] = staged[...].T
            k += 1


def _proj(x2d, g, w_bf16, tables, tm, seq=None):
    T, D = x2d.shape
    cos_t, sa_t, sb_t = tables
    pos_tiles = cos_t.shape[0] // tm
    transposed = seq is not None
    tab_spec = pl.BlockSpec((tm, GROUP_W), lambda i: (i % pos_tiles, 0))
    row_spec = pl.BlockSpec((tm, GROUP_W), lambda i: (i, 0))
    row_shape = jax.ShapeDtypeStruct((T, GROUP_W), F32)
    out_specs, out_shape = [], []
    for part in range(N_PARTS):
        if not transposed or part not in _T_ONLY:
            out_specs.append(row_spec)
            out_shape.append(row_shape)
        if transposed and part in _T_ONLY + _T_BOTH:
            tps = seq // tm
            out_specs.append(pl.BlockSpec((1, GROUP_W, tm), lambda i: (i // tps, 0, i % tps)))
            out_shape.append(jax.ShapeDtypeStruct((T // seq, GROUP_W, seq), F32))
    return pl.pallas_call(
        functools.partial(_proj_kernel, transposed=transposed),
        grid=(T // tm,),
        in_specs=[pl.BlockSpec((tm, D), lambda i: (i, 0)),
                  pl.BlockSpec((1, D), lambda i: (0, 0)),
                  pl.BlockSpec((D, N_PARTS * GROUP_W), lambda i: (0, 0)),
                  tab_spec, tab_spec, tab_spec],
        out_specs=out_specs,
        out_shape=out_shape,
        scratch_shapes=[pltpu.VMEM((tm, GROUP_W), F32)] if transposed else [],
        compiler_params=pltpu.CompilerParams(dimension_semantics=("arbitrary",), vmem_limit_bytes=VMEM_LIMIT),
        name="proj_t" if transposed else "proj",
    )(x2d, g, w_bf16, cos_t, sa_t, sb_t)


def _rope_tables(pos):
    inv = ROPE_THETA ** (-jnp.arange(0, ROT_DIM, 2, dtype=F32) / ROT_DIM)
    ang = pos.astype(F32)[:, None] * inv[None, :]
    cos, sin = jnp.cos(ang), jnp.sin(ang)
    n = pos.shape[0]
    pad = jnp.zeros((n, DIFF_QK - ROT_DIM), F32)
    zer = jnp.zeros((n, ROT_HALF), F32)
    reps = GROUP_W // DIFF_QK
    cos_t = jnp.tile(jnp.concatenate([cos, cos, pad + 1.0], axis=1), (1, reps))
    sa_t = jnp.tile(jnp.concatenate([-sin, zer, pad], axis=1), (1, reps))
    sb_t = jnp.tile(jnp.concatenate([zer, sin, pad], axis=1), (1, reps))
    return cos_t, sa_t, sb_t


def _pool_body(a, ext_ref, pos0, pw_ref, ps_ref):
    n = a.shape[0]
    lane_group = lax.broadcasted_iota(jnp.int32, (n, GROUP_W), 1) // POOL_CH
    pos1 = (pos0 + 1 + lax.broadcasted_iota(jnp.int32, (n, GROUP_W), 0)).astype(F32)
    run = a
    mean = None
    for sh in range(1, max(POOL_WINDOWS)):
        run = run + ext_ref[pl.ds(HALO - sh, n), :]
        if sh + 1 in POOL_WINDOWS:
            g = POOL_WINDOWS.index(sh + 1)
            m = run / jnp.minimum(pos1, float(sh + 1))
            mean = m if mean is None else jnp.where(lane_group == g, m, mean)
    pooled = (mean - a).astype(BF16)
    return jnp.dot(pooled, pw_ref[...], preferred_element_type=F32) * ps_ref[...]


def _pool_prompt_kernel(a_ref, pw_ref, ps_ref, o_ref, ext_ref, *, tm, tiles_per_seq):
    t = pl.program_id(0) % tiles_per_seq

    @pl.when(t == 0)
    def _():
        ext_ref[0:HALO, :] = jnp.zeros((HALO, GROUP_W), F32)

    @pl.when(t != 0)
    def _():
        ext_ref[0:HALO, :] = ext_ref[tm:tm + HALO, :]

    a = a_ref[...]
    ext_ref[HALO:HALO + tm, :] = a
    o_ref[...] = _pool_body(a, ext_ref, t * tm, pw_ref, ps_ref)


def _pool_sample_kernel(a_ref, halo_ref, pw_ref, ps_ref, o_ref, ext_ref, *, pos0):
    a = a_ref[0]
    ext_ref[0:HALO, :] = halo_ref[0]
    ext_ref[HALO:HALO + SUB, :] = a
    o_ref[0] = _pool_body(a, ext_ref, pos0, pw_ref, ps_ref)


def _pool_prompt(a2d, pw_bd, pscale, seq, tm):
    T = a2d.shape[0]
    return pl.pallas_call(
        functools.partial(_pool_prompt_kernel, tm=tm, tiles_per_seq=seq // tm),
        grid=(T // tm,),
        in_specs=[pl.BlockSpec((tm, GROUP_W), lambda i: (i, 0)),
                  pl.BlockSpec((GROUP_W, GROUP_W), lambda i: (0, 0)),
                  pl.BlockSpec((1, GROUP_W), lambda i: (0, 0))],
        out_specs=pl.BlockSpec((tm, GROUP_W), lambda i: (i, 0)),
        out_shape=jax.ShapeDtypeStruct((T, GROUP_W), F32),
        scratch_shapes=[pltpu.VMEM((tm + HALO, GROUP_W), F32)],
        compiler_params=pltpu.CompilerParams(dimension_semantics=("arbitrary",)),
        name="pool_prompt",
    )(a2d, pw_bd, pscale)


def _pool_sample(a3d, halo3d, pw_bd, pscale, pos0):
    nb = a3d.shape[0]
    return pl.pallas_call(
        functools.partial(_pool_sample_kernel, pos0=pos0),
        grid=(nb,),
        in_specs=[pl.BlockSpec((1, SUB, GROUP_W), lambda i: (i, 0, 0)),
                  pl.BlockSpec((1, HALO, GROUP_W), lambda i: (i, 0, 0)),
                  pl.BlockSpec((GROUP_W, GROUP_W), lambda i: (0, 0)),
                  pl.BlockSpec((1, GROUP_W), lambda i: (0, 0))],
        out_specs=pl.BlockSpec((1, SUB, GROUP_W), lambda i: (i, 0, 0)),
        out_shape=jax.ShapeDtypeStruct((nb, SUB, GROUP_W), F32),
        scratch_shapes=[pltpu.VMEM((HALO + SUB, GROUP_W), F32)],
        compiler_params=pltpu.CompilerParams(dimension_semantics=("arbitrary",)),
        name="pool_sample",
    )(a3d, halo3d, pw_bd, pscale)


def _sgu_kernel(su_ref, sv_ref, w_ref, b_ref, y_ref, v_ref, *, tm):
    row = lax.broadcasted_iota(jnp.int32, (CHUNK, CHUNK), 0)
    col = lax.broadcasted_iota(jnp.int32, (CHUNK, CHUNK), 1)
    tril = col <= row
    lane_group = lax.broadcasted_iota(jnp.int32, (CHUNK, GROUP_W), 1) // HEAD_DIM
    ws = [jnp.where(tril, w_ref[g], 0.0).astype(BF16) for g in range(N_HEADS)]
    for c in range(tm // CHUNK):
        rows = pl.ds(c * CHUNK, CHUNK)
        u = jax.nn.gelu(su_ref[rows, :])
        v = jax.nn.gelu(sv_ref[rows, :])
        v_ref[rows, :] = v
        vb = v.astype(BF16)
        mixed = None
        for g in range(N_HEADS):
            m = jnp.dot(ws[g], vb, preferred_element_type=F32)
            mixed = m if mixed is None else jnp.where(lane_group == g, m, mixed)
        y_ref[rows, :] = u * (mixed + b_ref[...])


def _sgu(su, sv, w, bias, tm):
    T = su.shape[0]
    row_spec = pl.BlockSpec((tm, GROUP_W), lambda i: (i, 0))
    return pl.pallas_call(
        functools.partial(_sgu_kernel, tm=tm),
        grid=(T // tm,),
        in_specs=[row_spec, row_spec,
                  pl.BlockSpec((N_HEADS, CHUNK, CHUNK), lambda i: (0, 0, 0)),
                  pl.BlockSpec((CHUNK, GROUP_W), lambda i: (0, 0))],
        out_specs=[row_spec, row_spec],
        out_shape=[jax.ShapeDtypeStruct((T, GROUP_W), F32)] * 2,
        compiler_params=pltpu.CompilerParams(dimension_semantics=("arbitrary",)),
        name="sgu",
    )(su, sv, w, bias)


def _diff_prompt_kernel(qt_ref, k_ref, vt_ref, lq1, lk1, lq2, lk2, g_ref, o_ref, qm_ref, m_ref, l_ref, acc_ref,
                        *, tq, lam_init):
    i = pl.program_id(1)
    n_maps = 2 * N_HEADS
    cexp = DIFF_QK ** -0.5 * math.log2(math.e)
    chan = lax.broadcasted_iota(jnp.int32, (GROUP_W, tq), 0) // DIFF_QK
    key = lax.broadcasted_iota(jnp.int32, (tq, tq), 0)
    qry = lax.broadcasted_iota(jnp.int32, (tq, tq), 1)
    qt = qt_ref[0]
    for c in range(n_maps):
        qm_ref[c] = jnp.where(chan == c, qt, 0.0).astype(BF16)
    m_ref[...] = jnp.full(m_ref.shape, NEG, F32)
    l_ref[...] = jnp.zeros(l_ref.shape, F32)
    acc_ref[...] = jnp.zeros(acc_ref.shape, F32)

    def step(off, mask):
        kt = k_ref[0, pl.ds(off, tq), :].astype(BF16)
        ss = [jnp.dot(kt, qm_ref[c], preferred_element_type=F32) for c in range(n_maps)]
        if mask is not None:
            ss = [jnp.where(mask, s, NEG) for s in ss]
        m_old = [m_ref[c] for c in range(n_maps)]
        m_new = [jnp.maximum(m_old[c], jnp.max(ss[c], axis=0, keepdims=True)) for c in range(n_maps)]
        alpha = [jnp.exp2((m_old[c] - m_new[c]) * cexp) for c in range(n_maps)]
        ps = [jnp.exp2((ss[c] - m_new[c]) * cexp) for c in range(n_maps)]
        for c in range(n_maps):
            l_ref[c] = alpha[c] * l_ref[c] + jnp.sum(ps[c], axis=0, keepdims=True)
            m_ref[c] = m_new[c]
        for h in range(N_HEADS):
            vt = vt_ref[0, h * HEAD_DIM:(h + 1) * HEAD_DIM, pl.ds(off, tq)].astype(BF16)
            for c in (2 * h, 2 * h + 1):
                acc_ref[c] = alpha[c] * acc_ref[c] + jnp.dot(vt, ps[c].astype(BF16), preferred_element_type=F32)

    def body(jb, carry):
        step(pl.multiple_of(jb * tq, tq), None)
        return carry

    lax.fori_loop(0, i, body, 0)
    step(pl.multiple_of(i * tq, tq), key <= qry)
    lam = _lam(lq1, lk1, lq2, lk2, lam_init)
    for h in range(N_HEADS):
        o = acc_ref[2 * h] / l_ref[2 * h] - lam * (acc_ref[2 * h + 1] / l_ref[2 * h + 1])
        o = o * lax.rsqrt(jnp.mean(o * o, axis=0, keepdims=True) + EPS) * g_ref[...] * (1.0 - lam_init)
        o_ref[0, h * HEAD_DIM:(h + 1) * HEAD_DIM, :] = o


def _sb_prompt_kernel(qt_ref, k_ref, vt_ref, o_ref, qm_ref, c_ref, acc_ref, *, tq):
    i = pl.program_id(1)
    czz = HEAD_DIM ** -0.5 * math.log2(math.e)
    chan = lax.broadcasted_iota(jnp.int32, (GROUP_W, tq), 0) // HEAD_DIM
    key = lax.broadcasted_iota(jnp.int32, (tq, tq), 0)
    qry = lax.broadcasted_iota(jnp.int32, (tq, tq), 1)
    neg_after = jnp.where(qry > key, -1.0, 0.0).astype(BF16)
    qt = qt_ref[0]
    for h in range(N_HEADS):
        qm_ref[h] = jnp.where(chan == h, qt, 0.0).astype(BF16)
    c_ref[...] = jnp.zeros(c_ref.shape, F32)
    acc_ref[...] = jnp.zeros(acc_ref.shape, F32)

    def step(off, mask):
        kt = k_ref[0, pl.ds(off, tq), :].astype(BF16)
        zs = [jnp.dot(kt, qm_ref[h], preferred_element_type=F32) * czz for h in range(N_HEADS)]
        sps = [jnp.maximum(z, 0.0) + jnp.log2(1.0 + jnp.exp2(-jnp.abs(z))) for z in zs]
        drop = sps if mask is None else [jnp.where(mask, sp, 0.0) for sp in sps]
        parts = [_split_bf16(d) for d in drop]
        rights = [jnp.dot(neg_after, hi, preferred_element_type=F32) + jnp.dot(neg_after, lo, preferred_element_type=F32)
                  for hi, lo in parts]
        for h in range(N_HEADS):
            a = jnp.exp2((zs[h] - sps[h]) + rights[h] + c_ref[h])
            if mask is not None:
                a = jnp.where(mask, a, 0.0)
            vt = vt_ref[0, h * HEAD_DIM:(h + 1) * HEAD_DIM, pl.ds(off, tq)].astype(BF16)
            acc_ref[h] += jnp.dot(vt, a.astype(BF16), preferred_element_type=F32)
            c_ref[h] -= jnp.sum(drop[h], axis=0, keepdims=True)

    step(pl.multiple_of(i * tq, tq), key < qry)

    def body(it, carry):
        step(pl.multiple_of((i - 1 - it) * tq, tq), None)
        return carry

    lax.fori_loop(0, i, body, 0)
    for h in range(N_HEADS):
        o_ref[0, h * HEAD_DIM:(h + 1) * HEAD_DIM, :] = acc_ref[h]


def _attn_specs(seq, tq):
    qt_spec = pl.BlockSpec((1, GROUP_W, tq), lambda b, i: (b, 0, i))
    k_spec = pl.BlockSpec((1, seq, GROUP_W), lambda b, i: (b, 0, 0))
    vt_spec = pl.BlockSpec((1, GROUP_W, seq), lambda b, i: (b, 0, 0))
    return qt_spec, k_spec, vt_spec


def _diff_prompt(qt, k, vt, lams, subln_col, lam_init, tq):
    batch, _, seq = qt.shape
    qt_spec, k_spec, vt_spec = _attn_specs(seq, tq)
    vec = pl.BlockSpec((1, DIFF_QK), lambda b, i: (0, 0))
    return pl.pallas_call(
        functools.partial(_diff_prompt_kernel, tq=tq, lam_init=lam_init),
        grid=(batch, seq // tq),
        in_specs=[qt_spec, k_spec, vt_spec, vec, vec, vec, vec, pl.BlockSpec((HEAD_DIM, 1), lambda b, i: (0, 0))],
        out_specs=qt_spec,
        out_shape=jax.ShapeDtypeStruct((batch, GROUP_W, seq), F32),
        scratch_shapes=[pltpu.VMEM((2 * N_HEADS, GROUP_W, tq), BF16),
                        pltpu.VMEM((2 * N_HEADS, 1, tq), F32), pltpu.VMEM((2 * N_HEADS, 1, tq), F32),
                        pltpu.VMEM((2 * N_HEADS, HEAD_DIM, tq), F32)],
        compiler_params=pltpu.CompilerParams(dimension_semantics=("arbitrary", "arbitrary"),
                                             vmem_limit_bytes=VMEM_LIMIT),
        name="diff_prompt",
    )(qt, k, vt, *lams, subln_col)


def _sb_prompt(qt, k, vt, tq):
    batch, _, seq = qt.shape
    qt_spec, k_spec, vt_spec = _attn_specs(seq, tq)
    return pl.pallas_call(
        functools.partial(_sb_prompt_kernel, tq=tq),
        grid=(batch, seq // tq),
        in_specs=[qt_spec, k_spec, vt_spec],
        out_specs=qt_spec,
        out_shape=jax.ShapeDtypeStruct((batch, GROUP_W, seq), F32),
        scratch_shapes=[pltpu.VMEM((N_HEADS, GROUP_W, tq), BF16),
                        pltpu.VMEM((N_HEADS, 1, tq), F32), pltpu.VMEM((N_HEADS, HEAD_DIM, tq), F32)],
        compiler_params=pltpu.CompilerParams(dimension_semantics=("arbitrary", "arbitrary"),
                                             vmem_limit_bytes=VMEM_LIMIT),
        name="sb_prompt",
    )(qt, k, vt)


def _decode_kernel(pt_ref, dq_ref, dk_ref, dv_ref, sq_ref, sk_ref, sv_ref, lq1, lk1, lq2, lk2, g_ref, *rest,
                   pages_per_step, n_new, lam_init):
    del pt_ref
    G = pages_per_step
    cdk = rest[0:G]
    cdv = rest[G:2 * G]
    csk = rest[2 * G:3 * G]
    csv = rest[3 * G:4 * G]
    yd_ref, ys_ref = rest[4 * G:4 * G + 2]
    qd_ref, qs_ref, pad_ref, m_ref, l_ref, accd_ref, c_ref, accs_ref = rest[4 * G + 2:]
    p = pl.program_id(1)
    nd = 2 * N_HEADS * SUB
    ns = N_HEADS * SUB
    d_scale = DIFF_QK ** -0.5
    s_scale = HEAD_DIM ** -0.5
    row = lax.broadcasted_iota(jnp.int32, (PAGE, 2 * PAGE), 0)
    col = lax.broadcasted_iota(jnp.int32, (PAGE, 2 * PAGE), 1)
    after_ones = jnp.where((row > col) | (col >= PAGE), 1.0, 0.0).astype(BF16)

    def attend(dks, dvs, sks, svs, dmask, smask):
        ss = [jnp.dot(qd_ref[...], kt, preferred_element_type=F32) * d_scale for kt in dks]
        zs = [jnp.dot(qs_ref[...], kt, preferred_element_type=F32) * s_scale for kt in sks]
        if dmask is not None:
            ss = [jnp.where(dmask, s, NEG) for s in ss]
        m_old = m_ref[...]
        m_new = jnp.maximum(m_old, jnp.max(functools.reduce(jnp.maximum, ss), axis=1, keepdims=True))
        alpha = jnp.exp(m_old - m_new)
        ps = [jnp.exp(s - m_new) for s in ss]
        if dmask is not None:
            ps = [jnp.where(dmask, pr, 0.0) for pr in ps]
        sps = [_softplus(z) for z in zs]
        log_keep = [-sp if smask is None else jnp.where(smask, -sp, 0.0) for sp in sps]
        parts = [_split_bf16(lk) for lk in log_keep]
        ecs = [jnp.dot(hi, after_ones, preferred_element_type=F32) + jnp.dot(lo, after_ones, preferred_element_type=F32)
               for hi, lo in parts]
        l_ref[...] = alpha * l_ref[...] + functools.reduce(jnp.add, ps)
        m_ref[...] = m_new
        upd = functools.reduce(jnp.add, [lax.dot_general(pr.astype(BF16), vt, _TRANS_B, preferred_element_type=F32)
                                         for pr, vt in zip(ps, dvs)])
        accd_ref[...] = alpha * accd_ref[...] + upd
        run = c_ref[...]
        probs = []
        for z, sp, ec in zip(zs, sps, ecs):
            a = jnp.exp((z - sp) + ec[:, :PAGE] + run)
            probs.append(a if smask is None else jnp.where(smask, a, 0.0))
            run = run + ec[:, PAGE:]
        c_ref[...] = run
        accs_ref[...] += functools.reduce(
            jnp.add, [lax.dot_general(a.astype(BF16), vt, _TRANS_B, preferred_element_type=F32)
                      for a, vt in zip(probs, svs)])

    @pl.when(p == 0)
    def _():
        lane = lax.broadcasted_iota(jnp.int32, (SUB, GROUP_W), 1)
        dq = dq_ref[0]
        sq = sq_ref[0]
        for j in range(2):
            for h in range(N_HEADS):
                sel = (lane // DIFF_QK) == (2 * h + j)
                r0 = (j * N_HEADS + h) * SUB
                qd_ref[r0:r0 + SUB, :] = jnp.where(sel, dq, 0.0).astype(BF16)
        for h in range(N_HEADS):
            sel = (lane // HEAD_DIM) == h
            qs_ref[h * SUB:(h + 1) * SUB, :] = jnp.where(sel, sq, 0.0).astype(BF16)
        m_ref[...] = jnp.full(m_ref.shape, NEG, F32)
        l_ref[...] = jnp.zeros(l_ref.shape, F32)
        accd_ref[...] = jnp.zeros(accd_ref.shape, F32)
        c_ref[...] = jnp.zeros(c_ref.shape, F32)
        accs_ref[...] = jnp.zeros(accs_ref.shape, F32)
        pad_ref[SUB:PAGE, :] = jnp.zeros((PAGE - SUB, GROUP_W), F32)
        tok_d = lax.broadcasted_iota(jnp.int32, (nd, PAGE), 0) % SUB
        key_d = lax.broadcasted_iota(jnp.int32, (nd, PAGE), 1)
        tok_s = lax.broadcasted_iota(jnp.int32, (ns, PAGE), 0) % SUB
        key_s = lax.broadcasted_iota(jnp.int32, (ns, PAGE), 1)

        def page_of(ref):
            pad_ref[0:SUB, :] = ref[0]
            return pad_ref[...].T.astype(BF16)

        k_new = page_of(dk_ref)
        v_new = page_of(dv_ref)
        sk_new = page_of(sk_ref)
        sv_new = page_of(sv_ref)
        attend([k_new], [v_new], [sk_new], [sv_new],
               (key_d <= tok_d) & (key_d < n_new), (key_s < tok_s) & (key_s < n_new))

    attend(*[[r[...].astype(BF16) for r in refs] for refs in (cdk, cdv, csk, csv)], None, None)

    @pl.when(p == pl.num_programs(1) - 1)
    def _():
        lam = _lam(lq1, lk1, lq2, lk2, lam_init)
        lane_head = lax.broadcasted_iota(jnp.int32, (SUB, GROUP_W), 1) // HEAD_DIM
        l_tot = jnp.sum(l_ref[...], axis=1, keepdims=True)
        yd = jnp.zeros((SUB, GROUP_W), F32)
        ys = jnp.zeros((SUB, GROUP_W), F32)
        for h in range(N_HEADS):
            r1 = h * SUB
            r2 = (N_HEADS + h) * SUB
            o = (accd_ref[r1:r1 + SUB, :] / l_tot[r1:r1 + SUB, :]
                 - lam * (accd_ref[r2:r2 + SUB, :] / l_tot[r2:r2 + SUB, :]))
            sel = lane_head == h
            ms = jnp.sum(jnp.where(sel, o * o, 0.0), axis=-1, keepdims=True) / HEAD_DIM
            yd = jnp.where(sel, o * lax.rsqrt(ms + EPS), yd)
            ys = jnp.where(sel, accs_ref[r1:r1 + SUB, :], ys)
        yd_ref[0] = yd * g_ref[...] * (1.0 - lam_init)
        ys_ref[0] = ys


def _decode(layer, page_table, new_rows, caches_t, lams, subln_tiled, lam_init, n_new, pages_per_step):
    nb, n_pages = page_table.shape
    G = pages_per_step
    steps = n_pages // G
    tok_spec = pl.BlockSpec((1, SUB, GROUP_W), lambda b, p, pt: (b, 0, 0))
    vec = pl.BlockSpec((1, DIFF_QK), lambda b, p, pt: (0, 0))

    def page_spec(g):
        return pl.BlockSpec((None, None, GROUP_W, PAGE),
                            lambda b, p, pt: (layer, pt[b, n_pages - 1 - (p * G + g)], 0, 0))

    page_specs = [page_spec(g) for g in range(G)]
    in_specs = ([tok_spec] * 6 + [vec] * 4 + [pl.BlockSpec((1, GROUP_W), lambda b, p, pt: (0, 0))]
                + page_specs * 4)
    cache_args = []
    for c in caches_t:
        cache_args += [c] * G
    nd = 2 * N_HEADS * SUB
    ns = N_HEADS * SUB
    return pl.pallas_call(
        functools.partial(_decode_kernel, pages_per_step=G, n_new=n_new, lam_init=lam_init),
        grid_spec=pltpu.PrefetchScalarGridSpec(
            num_scalar_prefetch=1,
            grid=(nb, steps),
            in_specs=in_specs,
            out_specs=[tok_spec, tok_spec],
            scratch_shapes=[pltpu.VMEM((nd, GROUP_W), BF16), pltpu.VMEM((ns, GROUP_W), BF16),
                            pltpu.VMEM((PAGE, GROUP_W), F32),
                            pltpu.VMEM((nd, 1), F32), pltpu.VMEM((nd, PAGE), F32), pltpu.VMEM((nd, GROUP_W), F32),
                            pltpu.VMEM((ns, PAGE), F32), pltpu.VMEM((ns, GROUP_W), F32)]),
        out_shape=[jax.ShapeDtypeStruct((nb, SUB, GROUP_W), F32)] * 2,
        compiler_params=pltpu.CompilerParams(dimension_semantics=("arbitrary", "arbitrary"),
                                             vmem_limit_bytes=VMEM_LIMIT),
        name="decode",
    )(page_table, *new_rows, *lams, subln_tiled, *cache_args)


def _tail_kernel(x_ref, yp_ref, yd_ref, ys_ref, yb_ref, p_ref, wout_ref, wup_ref, wdown_ref, wgate_ref, wpp_ref,
                 g_mix, g_pre, g_post, g_ple, o_ref, *, ff_chunk, attn_transposed):
    yd = yd_ref[0].T if attn_transposed else yd_ref[...]
    yb = yb_ref[0].T if attn_transposed else yb_ref[...]
    mix = jnp.concatenate([yp_ref[...], yd, ys_ref[...], yb], axis=-1).astype(BF16)
    x = x_ref[...] + _rms(jnp.dot(mix, wout_ref[...], preferred_element_type=F32), g_mix[...])
    h = _rms(x, g_pre[...]).astype(BF16)
    f = None
    for c in range(wup_ref.shape[1] // ff_chunk):
        cols = slice(c * ff_chunk, (c + 1) * ff_chunk)
        u = jnp.dot(h, wup_ref[:, cols], preferred_element_type=F32)
        u = jnp.square(jnp.maximum(u, 0.0)).astype(BF16)
        d = jnp.dot(u, wdown_ref[cols, :], preferred_element_type=F32)
        f = d if f is None else f + d
    x = x + _rms(f, g_post[...])
    gate = jax.nn.sigmoid(jnp.dot(x.astype(BF16), wgate_ref[...], preferred_element_type=F32))
    pp = jnp.dot(p_ref[...].astype(BF16), wpp_ref[...], preferred_element_type=F32)
    o_ref[...] = x + _rms(gate * pp, g_ple[...])


def _tail(x2d, ys, p2d, weights, gains, tm, seq=None):
    T, D = x2d.shape
    wout, wup, wdown, wgate, wpp = weights
    dff = wup.shape[1]
    attn_transposed = seq is not None

    def resident(shape):
        return pl.BlockSpec(shape, lambda i: (0,) * len(shape), pipeline_mode=pl.Buffered(1))

    row = lambda w: pl.BlockSpec((tm, w), lambda i: (i, 0))
    if attn_transposed:
        tps = seq // tm
        attn = pl.BlockSpec((1, GROUP_W, tm), lambda i: (i // tps, 0, i % tps))
    else:
        attn = row(GROUP_W)
    return pl.pallas_call(
        functools.partial(_tail_kernel, ff_chunk=min(dff, 1024), attn_transposed=attn_transposed),
        grid=(T // tm,),
        in_specs=[row(D), row(GROUP_W), attn, row(GROUP_W), attn, row(p2d.shape[1]),
                  resident(wout.shape), resident(wup.shape), resident(wdown.shape), resident(wgate.shape),
                  resident(wpp.shape)] + [resident((1, D))] * 4,
        out_specs=row(D),
        out_shape=jax.ShapeDtypeStruct((T, D), F32),
        compiler_params=pltpu.CompilerParams(dimension_semantics=("arbitrary",), vmem_limit_bytes=VMEM_LIMIT),
        name="tail_t" if attn_transposed else "tail",
    )(x2d, *ys, p2d, wout, wup, wdown, wgate, wpp, *gains)


def _block_diag(w):
    n, c, _ = w.shape
    eye = jnp.eye(n, dtype=w.dtype)
    return (eye[:, None, :, None] * w[:, :, None, :]).reshape(n * c, n * c)


def _channel_major_pages(cache):
    nd = cache.ndim
    perm = (0, 1) + tuple(range(3, nd)) + (2,)
    return jnp.transpose(cache, perm).reshape(cache.shape[0], cache.shape[1], GROUP_W, PAGE)


def _rows_from_channel_major(t, tail_shape):
    b, _, s = t.shape
    n = len(tail_shape)
    return jnp.transpose(t.reshape((b,) + tail_shape + (s,)), (0, n + 1) + tuple(range(1, n + 1)))


def kernel(x_prompt, x_sample, state_pool, cache_diff_k, cache_diff_v, cache_sb_k, cache_sb_v, page_table,
           p_prompt, p_sample, w_in, w_out, norm_mix_pre, norm_mix_post, norm_ffn_pre, norm_ffn_post, norm_ple,
           pool_w, pool_scale, diff_lam_q1, diff_lam_k1, diff_lam_q2, diff_lam_k2, diff_subln,
           sgu_w, sgu_b, w_ffn_up, w_ffn_down, w_ple_gate, w_ple_proj):
    batch, seq, d_model = x_prompt.shape
    nb, dec_seq, _ = x_sample.shape
    depth = w_in.shape[0]
    n_pages = page_table.shape[1]
    past_len = n_pages * PAGE
    ple_dim = p_prompt.shape[-1]
    assert d_model == 4 * GROUP_W and dec_seq <= SUB and seq % CHUNK == 0 and (nb * SUB) % CHUNK == 0

    tm = min(512, seq)
    tq = min(256, seq)
    ts = nb * SUB
    pages_per_step = 8 if n_pages % 8 == 0 else 1

    tables_p = _rope_tables(jnp.arange(seq))
    tables_s = _rope_tables(jnp.tile(past_len + jnp.arange(SUB), nb))

    xp = x_prompt.reshape(batch * seq, d_model)
    pad_s = lambda t: jnp.pad(t, ((0, 0), (0, SUB - dec_seq), (0, 0))).reshape(ts, -1)
    xs = pad_s(x_sample)
    caches_t = [_channel_major_pages(c) for c in (cache_diff_k, cache_diff_v, cache_sb_k, cache_sb_v)]

    outs = {name: [] for name in ("pool_p", "pool_s", "dk_p", "dv_p", "sk_p", "sv_p",
                                  "dk_s", "dv_s", "sk_s", "sv_s", "sgu_s")}
    unpad = lambda t: t.reshape(nb, SUB, GROUP_W)[:, :dec_seq]
    qk_shape = (N_HEADS, 2, DIFF_QK)
    v_shape = (N_HEADS, HEAD_DIM)

    for i in range(depth):
        lam_init = 0.8 - 0.6 * math.exp(-0.3 * i)
        w_in_b = w_in[i].astype(BF16)
        weights = tuple(w[i].astype(BF16) for w in (w_out, w_ffn_up, w_ffn_down, w_ple_gate, w_ple_proj))
        gains = tuple(g[i].reshape(1, d_model) for g in (norm_mix_post, norm_ffn_pre, norm_ffn_post, norm_ple))
        g_pre = norm_mix_pre[i].reshape(1, d_model)
        pw_bd = _block_diag(pool_w[i]).astype(BF16)
        pscale = pool_scale[i].reshape(1, GROUP_W)
        lams = tuple(v[i].reshape(1, DIFF_QK) for v in (diff_lam_q1, diff_lam_k1, diff_lam_q2, diff_lam_k2))
        subln = diff_subln[i].reshape(1, HEAD_DIM)
        sgu_bias_p = jnp.repeat(sgu_b[i].T, HEAD_DIM, axis=1)
        reps = CHUNK // SUB
        sgu_w_s = jax.vmap(lambda w: jnp.kron(jnp.eye(reps, dtype=w.dtype), w[:SUB, :SUB]))(sgu_w[i])
        sgu_bias_s = jnp.tile(jnp.repeat(sgu_b[i][:, :SUB].T, HEAD_DIM, axis=1), (reps, 1))

        a, dqt, dk, dkt, dvt, su, sv, sqt, sk, skt, svt = _proj(xp, g_pre, w_in_b, tables_p, tm, seq=seq)
        y_pool = _pool_prompt(a, pw_bd, pscale, seq, tm)
        y_sgu, _ = _sgu(su, sv, sgu_w[i], sgu_bias_p, tm)
        y_diff_t = _diff_prompt(dqt, dk.reshape(batch, seq, GROUP_W), dvt, lams, subln.reshape(HEAD_DIM, 1),
                                lam_init, tq)
        y_sb_t = _sb_prompt(sqt, sk.reshape(batch, seq, GROUP_W), svt, tq)
        xp = _tail(xp, (y_pool, y_diff_t, y_sgu, y_sb_t), p_prompt[i].reshape(batch * seq, ple_dim), weights, gains,
                   tm, seq=seq)
        outs["pool_p"].append(a.reshape(batch, seq, GROUP_W)[:, seq - POOL_BUF:])
        outs["dk_p"].append(_rows_from_channel_major(dkt, qk_shape))
        outs["dv_p"].append(_rows_from_channel_major(dvt, v_shape))
        outs["sk_p"].append(_rows_from_channel_major(skt, v_shape))
        outs["sv_p"].append(_rows_from_channel_major(svt, v_shape))

        a, dq, dk, dv, su, sv, sq, sk, svv = _proj(xs, g_pre, w_in_b, tables_s, ts)
        halo = jnp.concatenate([jnp.zeros((nb, HALO - POOL_BUF, GROUP_W), F32), state_pool[i]], axis=1)
        y_pool = _pool_sample(a.reshape(nb, SUB, GROUP_W), halo, pw_bd, pscale, past_len).reshape(ts, GROUP_W)
        y_sgu, sgu_v = _sgu(su, sv, sgu_w_s, sgu_bias_s, CHUNK)
        new_rows = tuple(t.reshape(nb, SUB, GROUP_W) for t in (dq, dk, dv, sq, sk, svv))
        y_diff, y_sb = _decode(i, page_table, new_rows, caches_t, lams, jnp.tile(subln, (1, N_HEADS)), lam_init,
                               dec_seq, pages_per_step)
        xs = _tail(xs, (y_pool, y_diff.reshape(ts, GROUP_W), y_sgu, y_sb.reshape(ts, GROUP_W)),
                   pad_s(p_sample[i]), weights, gains, ts)
        outs["pool_s"].append(jnp.concatenate([state_pool[i], unpad(a)], axis=1)[:, -POOL_BUF:])
        outs["dk_s"].append(unpad(dk).reshape((nb, dec_seq) + qk_shape))
        outs["dv_s"].append(unpad(dv).reshape((nb, dec_seq) + v_shape))
        outs["sk_s"].append(unpad(sk).reshape((nb, dec_seq) + v_shape))
        outs["sv_s"].append(unpad(svv).reshape((nb, dec_seq) + v_shape))
        outs["sgu_s"].append(unpad(sgu_v))

    st = lambda name: jnp.stack(outs[name])
    return (xp.reshape(batch, seq, d_model), xs.reshape(nb, SUB, d_model)[:, :dec_seq],
            st("pool_p"), st("pool_s"),
            st("dk_p"), st("dv_p"), st("sk_p"), st("sv_p"),
            st("dk_s"), st("dv_s"), st("sk_s"), st("sv_s"),
            st("sgu_s"))
```

```python
import functools
import math

import jax
import jax.numpy as jnp
from jax import lax
from jax.experimental import pallas as pl
from jax.experimental.pallas import tpu as pltpu

F32 = jnp.float32
BF16 = jnp.bfloat16

EPS = 1e-6
GROUP_W = 256
N_PARTS = 9
HEAD_DIM = 64
N_HEADS = GROUP_W // HEAD_DIM
DIFF_QK = HEAD_DIM // 2
ROT_DIM = DIFF_QK // 4
ROT_HALF = ROT_DIM // 2
ROPE_THETA = 500000.0
POOL_WINDOWS = (2, 4, 8, 16)
POOL_CH = GROUP_W // len(POOL_WINDOWS)
POOL_BUF = max(POOL_WINDOWS) - 1
HALO = POOL_BUF + 1
CHUNK = 128
PAGE = 128
SUB = 8
NEG = -1e30
VMEM_LIMIT = 56 * 1024 * 1024

_TRANS_B = (((1,), (1,)), ((), ()))


def _rms(x, g):
    return x * lax.rsqrt(jnp.mean(x * x, axis=-1, keepdims=True) + EPS) * g


def _softplus(z):
    return jnp.maximum(z, 0.0) + jnp.log(1.0 + jnp.exp(-jnp.abs(z)))


def _lam(lq1, lk1, lq2, lk2, lam_init):
    return (jnp.exp(jnp.sum(lq1[...] * lk1[...], axis=-1, keepdims=True))
            - jnp.exp(jnp.sum(lq2[...] * lk2[...], axis=-1, keepdims=True)) + lam_init)


def _split_bf16(x):
    hi = x.astype(BF16)
    lo = (x - hi.astype(F32)).astype(BF16)
    return hi, lo


_T_ONLY = (1, 3, 6, 8)
_T_BOTH = (2, 7)


def _proj_kernel(x_ref, g_ref, w_ref, cos_ref, sa_ref, sb_ref, *refs, transposed):
    out_refs = refs[:-1] if transposed else refs
    xn = _rms(x_ref[...], g_ref[...]).astype(BF16)
    k = 0
    for part in range(N_PARTS):
        y = jnp.dot(xn, w_ref[:, part * GROUP_W:(part + 1) * GROUP_W], preferred_element_type=F32)
        if part in (1, 2):
            y = (y * cos_ref[...] + pltpu.roll(y, GROUP_W - ROT_HALF, 1) * sa_ref[...]
                 + pltpu.roll(y, ROT_HALF, 1) * sb_ref[...])
        staged = refs[-1]
        if not transposed or part not in _T_ONLY:
            staged = out_refs[k]
            k += 1
        staged[...] = y
        if transposed and part in _T_ONLY + _T_BOTH:
            out_refs[k][0] = staged[...].T
            k += 1


def _proj(x2d, g, w_bf16, tables, tm, seq=None):
    T, D = x2d.shape
    cos_t, sa_t, sb_t = tables
    pos_tiles = cos_t.shape[0] // tm
    transposed = seq is not None
    tab_spec = pl.BlockSpec((tm, GROUP_W), lambda i: (i % pos_tiles, 0))
    row_spec = pl.BlockSpec((tm, GROUP_W), lambda i: (i, 0))
    row_shape = jax.ShapeDtypeStruct((T, GROUP_W), F32)
    out_specs, out_shape = [], []
    for part in range(N_PARTS):
        if not transposed or part not in _T_ONLY:
            out_specs.append(row_spec)
            out_shape.append(row_shape)
        if transposed and part in _T_ONLY + _T_BOTH:
            tps = seq // tm
            out_specs.append(pl.BlockSpec((1, GROUP_W, tm), lambda i: (i // tps, 0, i % tps)))
            out_shape.append(jax.ShapeDtypeStruct((T // seq, GROUP_W, seq), F32))
    return pl.pallas_call(
        functools.partial(_proj_kernel, transposed=transposed),
        grid=(T // tm,),
        in_specs=[pl.BlockSpec((tm, D), lambda i: (i, 0)),
                  pl.BlockSpec((1, D), lambda i: (0, 0)),
                  pl.BlockSpec((D, N_PARTS * GROUP_W), lambda i: (0, 0)),
                  tab_spec, tab_spec, tab_spec],
        out_specs=out_specs,
        out_shape=out_shape,
        scratch_shapes=[pltpu.VMEM((tm, GROUP_W), F32)] if transposed else [],
        compiler_params=pltpu.CompilerParams(dimension_semantics=("arbitrary",), vmem_limit_bytes=VMEM_LIMIT),
        name="proj_t" if transposed else "proj",
    )(x2d, g, w_bf16, cos_t, sa_t, sb_t)


def _rope_tables(pos):
    inv = ROPE_THETA ** (-jnp.arange(0, ROT_DIM, 2, dtype=F32) / ROT_DIM)
    ang = pos.astype(F32)[:, None] * inv[None, :]
    cos, sin = jnp.cos(ang), jnp.sin(ang)
    n = pos.shape[0]
    pad = jnp.zeros((n, DIFF_QK - ROT_DIM), F32)
    zer = jnp.zeros((n, ROT_HALF), F32)
    reps = GROUP_W // DIFF_QK
    cos_t = jnp.tile(jnp.concatenate([cos, cos, pad + 1.0], axis=1), (1, reps))
    sa_t = jnp.tile(jnp.concatenate([-sin, zer, pad], axis=1), (1, reps))
    sb_t = jnp.tile(jnp.concatenate([zer, sin, pad], axis=1), (1, reps))
    return cos_t, sa_t, sb_t


def _pool_body(a, ext_ref, pos0, pw_ref, ps_ref):
    n = a.shape[0]
    lane_group = lax.broadcasted_iota(jnp.int32, (n, GROUP_W), 1) // POOL_CH
    pos1 = (pos0 + 1 + lax.broadcasted_iota(jnp.int32, (n, GROUP_W), 0)).astype(F32)
    run = a
    mean = None
    for sh in range(1, max(POOL_WINDOWS)):
        run = run + ext_ref[pl.ds(HALO - sh, n), :]
        if sh + 1 in POOL_WINDOWS:
            g = POOL_WINDOWS.index(sh + 1)
            m = run / jnp.minimum(pos1, float(sh + 1))
            mean = m if mean is None else jnp.where(lane_group == g, m, mean)
    pooled = (mean - a).astype(BF16)
    return jnp.dot(pooled, pw_ref[...], preferred_element_type=F32) * ps_ref[...]


def _pool_prompt_kernel(a_ref, pw_ref, ps_ref, o_ref, ext_ref, *, tm, tiles_per_seq):
    t = pl.program_id(0) % tiles_per_seq

    @pl.when(t == 0)
    def _():
        ext_ref[0:HALO, :] = jnp.zeros((HALO, GROUP_W), F32)

    @pl.when(t != 0)
    def _():
        ext_ref[0:HALO, :] = ext_ref[tm:tm + HALO, :]

    a = a_ref[...]
    ext_ref[HALO:HALO + tm, :] = a
    o_ref[...] = _pool_body(a, ext_ref, t * tm, pw_ref, ps_ref)


def _pool_sample_kernel(a_ref, halo_ref, pw_ref, ps_ref, o_ref, ext_ref, *, pos0):
    a = a_ref[0]
    ext_ref[0:HALO, :] = halo_ref[0]
    ext_ref[HALO:HALO + SUB, :] = a
    o_ref[0] = _pool_body(a, ext_ref, pos0, pw_ref, ps_ref)


def _pool_prompt(a2d, pw_bd, pscale, seq, tm):
    T = a2d.shape[0]
    return pl.pallas_call(
        functools.partial(_pool_prompt_kernel, tm=tm, tiles_per_seq=seq // tm),
        grid=(T // tm,),
        in_specs=[pl.BlockSpec((tm, GROUP_W), lambda i: (i, 0)),
                  pl.BlockSpec((GROUP_W, GROUP_W), lambda i: (0, 0)),
                  pl.BlockSpec((1, GROUP_W), lambda i: (0, 0))],
        out_specs=pl.BlockSpec((tm, GROUP_W), lambda i: (i, 0)),
        out_shape=jax.ShapeDtypeStruct((T, GROUP_W), F32),
        scratch_shapes=[pltpu.VMEM((tm + HALO, GROUP_W), F32)],
        compiler_params=pltpu.CompilerParams(dimension_semantics=("arbitrary",)),
        name="pool_prompt",
    )(a2d, pw_bd, pscale)


def _pool_sample(a3d, halo3d, pw_bd, pscale, pos0):
    nb = a3d.shape[0]
    return pl.pallas_call(
        functools.partial(_pool_sample_kernel, pos0=pos0),
        grid=(nb,),
        in_specs=[pl.BlockSpec((1, SUB, GROUP_W), lambda i: (i, 0, 0)),
                  pl.BlockSpec((1, HALO, GROUP_W), lambda i: (i, 0, 0)),
                  pl.BlockSpec((GROUP_W, GROUP_W), lambda i: (0, 0)),
                  pl.BlockSpec((1, GROUP_W), lambda i: (0, 0))],
        out_specs=pl.BlockSpec((1, SUB, GROUP_W), lambda i: (i, 0, 0)),
        out_shape=jax.ShapeDtypeStruct((nb, SUB, GROUP_W), F32),
        scratch_shapes=[pltpu.VMEM((HALO + SUB, GROUP_W), F32)],
        compiler_params=pltpu.CompilerParams(dimension_semantics=("arbitrary",)),
        name="pool_sample",
    )(a3d, halo3d, pw_bd, pscale)


def _sgu_kernel(su_ref, sv_ref, w_ref, b_ref, y_ref, v_ref, *, tm):
    row = lax.broadcasted_iota(jnp.int32, (CHUNK, CHUNK), 0)
    col = lax.broadcasted_iota(jnp.int32, (CHUNK, CHUNK), 1)
    tril = col <= row
    lane_group = lax.broadcasted_iota(jnp.int32, (CHUNK, GROUP_W), 1) // HEAD_DIM
    ws = [jnp.where(tril, w_ref[g], 0.0).astype(BF16) for g in range(N_HEADS)]
    for c in range(tm // CHUNK):
        rows = pl.ds(c * CHUNK, CHUNK)
        u = jax.nn.gelu(su_ref[rows, :])
        v = jax.nn.gelu(sv_ref[rows, :])
        v_ref[rows, :] = v
        vb = v.astype(BF16)
        mixed = None
        for g in range(N_HEADS):
            m = jnp.dot(ws[g], vb, preferred_element_type=F32)
            mixed = m if mixed is None else jnp.where(lane_group == g, m, mixed)
        y_ref[rows, :] = u * (mixed + b_ref[...])


def _sgu(su, sv, w, bias, tm):
    T = su.shape[0]
    row_spec = pl.BlockSpec((tm, GROUP_W), lambda i: (i, 0))
    return pl.pallas_call(
        functools.partial(_sgu_kernel, tm=tm),
        grid=(T // tm,),
        in_specs=[row_spec, row_spec,
                  pl.BlockSpec((N_HEADS, CHUNK, CHUNK), lambda i: (0, 0, 0)),
                  pl.BlockSpec((CHUNK, GROUP_W), lambda i: (0, 0))],
        out_specs=[row_spec, row_spec],
        out_shape=[jax.ShapeDtypeStruct((T, GROUP_W), F32)] * 2,
        compiler_params=pltpu.CompilerParams(dimension_semantics=("arbitrary",)),
        name="sgu",
    )(su, sv, w, bias)


def _diff_prompt_kernel(qt_ref, k_ref, vt_ref, lq1, lk1, lq2, lk2, g_ref, o_ref, qm_ref, m_ref, l_ref, acc_ref,
                        *, tq, lam_init):
    i = pl.program_id(1)
    n_maps = 2 * N_HEADS
    cexp = DIFF_QK ** -0.5 * math.log2(math.e)
    chan = lax.broadcasted_iota(jnp.int32, (GROUP_W, tq), 0) // DIFF_QK
    key = lax.broadcasted_iota(jnp.int32, (tq, tq), 0)
    qry = lax.broadcasted_iota(jnp.int32, (tq, tq), 1)
    qt = qt_ref[0]
    for c in range(n_maps):
        qm_ref[c] = jnp.where(chan == c, qt, 0.0).astype(BF16)
    m_ref[...] = jnp.full(m_ref.shape, NEG, F32)
    l_ref[...] = jnp.zeros(l_ref.shape, F32)
    acc_ref[...] = jnp.zeros(acc_ref.shape, F32)

    def step(off, mask):
        kt = k_ref[0, pl.ds(off, tq), :].astype(BF16)
        ss = [jnp.dot(kt, qm_ref[c], preferred_element_type=F32) for c in range(n_maps)]
        if mask is not None:
            ss = [jnp.where(mask, s, NEG) for s in ss]
        m_old = [m_ref[c] for c in range(n_maps)]
        m_new = [jnp.maximum(m_old[c], jnp.max(ss[c], axis=0, keepdims=True)) for c in range(n_maps)]
        alpha = [jnp.exp2((m_old[c] - m_new[c]) * cexp) for c in range(n_maps)]
        ps = [jnp.exp2((ss[c] - m_new[c]) * cexp) for c in range(n_maps)]
        for c in range(n_maps):
            l_ref[c] = alpha[c] * l_ref[c] + jnp.sum(ps[c], axis=0, keepdims=True)
            m_ref[c] = m_new[c]
        for h in range(N_HEADS):
            vt = vt_ref[0, h * HEAD_DIM:(h + 1) * HEAD_DIM, pl.ds(off, tq)].astype(BF16)
            for c in (2 * h, 2 * h + 1):
                acc_ref[c] = alpha[c] * acc_ref[c] + jnp.dot(vt, ps[c].astype(BF16), preferred_element_type=F32)

    def body(jb, carry):
        step(pl.multiple_of(jb * tq, tq), None)
        return carry

    lax.fori_loop(0, i, body, 0)
    step(pl.multiple_of(i * tq, tq), key <= qry)
    lam = _lam(lq1, lk1, lq2, lk2, lam_init)
    for h in range(N_HEADS):
        o = acc_ref[2 * h] / l_ref[2 * h] - lam * (acc_ref[2 * h + 1] / l_ref[2 * h + 1])
        o = o * lax.rsqrt(jnp.mean(o * o, axis=0, keepdims=True) + EPS) * g_ref[...] * (1.0 - lam_init)
        o_ref[0, h * HEAD_DIM:(h + 1) * HEAD_DIM, :] = o


def _sb_prompt_kernel(qt_ref, k_ref, vt_ref, o_ref, qm_ref, c_ref, acc_ref, *, tq):
    i = pl.program_id(1)
    czz = HEAD_DIM ** -0.5 * math.log2(math.e)
    chan = lax.broadcasted_iota(jnp.int32, (GROUP_W, tq), 0) // HEAD_DIM
    key = lax.broadcasted_iota(jnp.int32, (tq, tq), 0)
    qry = lax.broadcasted_iota(jnp.int32, (tq, tq), 1)
    neg_after = jnp.where(qry > key, -1.0, 0.0).astype(BF16)
    qt = qt_ref[0]
    for h in range(N_HEADS):
        qm_ref[h] = jnp.where(chan == h, qt, 0.0).astype(BF16)
    c_ref[...] = jnp.zeros(c_ref.shape, F32)
    acc_ref[...] = jnp.zeros(acc_ref.shape, F32)

    def step(off, mask):
        kt = k_ref[0, pl.ds(off, tq), :].astype(BF16)
        zs = [jnp.dot(kt, qm_ref[h], preferred_element_type=F32) * czz for h in range(N_HEADS)]
        sps = [jnp.maximum(z, 0.0) + jnp.log2(1.0 + jnp.exp2(-jnp.abs(z))) for z in zs]
        drop = sps if mask is None else [jnp.where(mask, sp, 0.0) for sp in sps]
        parts = [_split_bf16(d) for d in drop]
        rights = [jnp.dot(neg_after, hi, preferred_element_type=F32) + jnp.dot(neg_after, lo, preferred_element_type=F32)
                  for hi, lo in parts]
        for h in range(N_HEADS):
            a = jnp.exp2((zs[h] - sps[h]) + rights[h] + c_ref[h])
            if mask is not None:
                a = jnp.where(mask, a, 0.0)
            vt = vt_ref[0, h * HEAD_DIM:(h + 1) * HEAD_DIM, pl.ds(off, tq)].astype(BF16)
            acc_ref[h] += jnp.dot(vt, a.astype(BF16), preferred_element_type=F32)
            c_ref[h] -= jnp.sum(drop[h], axis=0, keepdims=True)

    step(pl.multiple_of(i * tq, tq), key < qry)

    def body(it, carry):
        step(pl.multiple_of((i - 1 - it) * tq, tq), None)
        return carry

    lax.fori_loop(0, i, body, 0)
    for h in range(N_HEADS):
        o_ref[0, h * HEAD_DIM:(h + 1) * HEAD_DIM, :] = acc_ref[h]


def _attn_specs(seq, tq):
    qt_spec = pl.BlockSpec((1, GROUP_W, tq), lambda b, i: (b, 0, i))
    k_spec = pl.BlockSpec((1, seq, GROUP_W), lambda b, i: (b, 0, 0))
    vt_spec = pl.BlockSpec((1, GROUP_W, seq), lambda b, i: (b, 0, 0))
    return qt_spec, k_spec, vt_spec


def _diff_prompt(qt, k, vt, lams, subln_col, lam_init, tq):
    batch, _, seq = qt.shape
    qt_spec, k_spec, vt_spec = _attn_specs(seq, tq)
    vec = pl.BlockSpec((1, DIFF_QK), lambda b, i: (0, 0))
    return pl.pallas_call(
        functools.partial(_diff_prompt_kernel, tq=tq, lam_init=lam_init),
        grid=(batch, seq // tq),
        in_specs=[qt_spec, k_spec, vt_spec, vec, vec, vec, vec, pl.BlockSpec((HEAD_DIM, 1), lambda b, i: (0, 0))],
        out_specs=qt_spec,
        out_shape=jax.ShapeDtypeStruct((batch, GROUP_W, seq), F32),
        scratch_shapes=[pltpu.VMEM((2 * N_HEADS, GROUP_W, tq), BF16),
                        pltpu.VMEM((2 * N_HEADS, 1, tq), F32), pltpu.VMEM((2 * N_HEADS, 1, tq), F32),
                        pltpu.VMEM((2 * N_HEADS, HEAD_DIM, tq), F32)],
        compiler_params=pltpu.CompilerParams(dimension_semantics=("arbitrary", "arbitrary"),
                                             vmem_limit_bytes=VMEM_LIMIT),
        name="diff_prompt",
    )(qt, k, vt, *lams, subln_col)


def _sb_prompt(qt, k, vt, tq):
    batch, _, seq = qt.shape
    qt_spec, k_spec, vt_spec = _attn_specs(seq, tq)
    return pl.pallas_call(
        functools.partial(_sb_prompt_kernel, tq=tq),
        grid=(batch, seq // tq),
        in_specs=[qt_spec, k_spec, vt_spec],
        out_specs=qt_spec,
        out_shape=jax.ShapeDtypeStruct((batch, GROUP_W, seq), F32),
        scratch_shapes=[pltpu.VMEM((N_HEADS, GROUP_W, tq), BF16),
                        pltpu.VMEM((N_HEADS, 1, tq), F32), pltpu.VMEM((N_HEADS, HEAD_DIM, tq), F32)],
        compiler_params=pltpu.CompilerParams(dimension_semantics=("arbitrary", "arbitrary"),
                                             vmem_limit_bytes=VMEM_LIMIT),
        name="sb_prompt",
    )(qt, k, vt)


def _decode_fns(tok_refs, lam_refs, g_ref, yd_ref, ys_ref, state, *, n_new, lam_init):
    dq_ref, dk_ref, dv_ref, sq_ref, sk_ref, sv_ref = tok_refs
    qd_ref, qs_ref, pad_ref, m_ref, l_ref, accd_ref, c_ref, accs_ref = state
    nd = 2 * N_HEADS * SUB
    ns = N_HEADS * SUB
    d_scale = DIFF_QK ** -0.5
    s_scale = HEAD_DIM ** -0.5

    def attend(dks, dvs, sks, svs, dmask, smask):
        row = lax.broadcasted_iota(jnp.int32, (PAGE, 2 * PAGE), 0)
        col = lax.broadcasted_iota(jnp.int32, (PAGE, 2 * PAGE), 1)
        after_ones = jnp.where((row > col) | (col >= PAGE), 1.0, 0.0).astype(BF16)
        ss = [jnp.dot(qd_ref[...], kt, preferred_element_type=F32) * d_scale for kt in dks]
        zs = [jnp.dot(qs_ref[...], kt, preferred_element_type=F32) * s_scale for kt in sks]
        if dmask is not None:
            ss = [jnp.where(dmask, s, NEG) for s in ss]
        m_old = m_ref[...]
        m_new = jnp.maximum(m_old, jnp.max(functools.reduce(jnp.maximum, ss), axis=1, keepdims=True))
        alpha = jnp.exp(m_old - m_new)
        ps = [jnp.exp(s - m_new) for s in ss]
        if dmask is not None:
            ps = [jnp.where(dmask, pr, 0.0) for pr in ps]
        sps = [_softplus(z) for z in zs]
        log_keep = [-sp if smask is None else jnp.where(smask, -sp, 0.0) for sp in sps]
        parts = [_split_bf16(lk) for lk in log_keep]
        ecs = [jnp.dot(hi, after_ones, preferred_element_type=F32) + jnp.dot(lo, after_ones, preferred_element_type=F32)
               for hi, lo in parts]
        l_ref[...] = alpha * l_ref[...] + functools.reduce(jnp.add, ps)
        m_ref[...] = m_new
        upd = functools.reduce(jnp.add, [lax.dot_general(pr.astype(BF16), vt, _TRANS_B, preferred_element_type=F32)
                                         for pr, vt in zip(ps, dvs)])
        accd_ref[...] = alpha * accd_ref[...] + upd
        run = c_ref[...]
        probs = []
        for z, sp, ec in zip(zs, sps, ecs):
            a = jnp.exp((z - sp) + ec[:, :PAGE] + run)
            probs.append(a if smask is None else jnp.where(smask, a, 0.0))
            run = run + ec[:, PAGE:]
        c_ref[...] = run
        accs_ref[...] += functools.reduce(
            jnp.add, [lax.dot_general(a.astype(BF16), vt, _TRANS_B, preferred_element_type=F32)
                      for a, vt in zip(probs, svs)])

    def begin(q):
        lane = lax.broadcasted_iota(jnp.int32, (SUB, GROUP_W), 1)
        dq = dq_ref[q]
        sq = sq_ref[q]
        for j in range(2):
            for h in range(N_HEADS):
                sel = (lane // DIFF_QK) == (2 * h + j)
                r0 = (j * N_HEADS + h) * SUB
                qd_ref[r0:r0 + SUB, :] = jnp.where(sel, dq, 0.0).astype(BF16)
        for h in range(N_HEADS):
            sel = (lane // HEAD_DIM) == h
            qs_ref[h * SUB:(h + 1) * SUB, :] = jnp.where(sel, sq, 0.0).astype(BF16)
        m_ref[...] = jnp.full(m_ref.shape, NEG, F32)
        l_ref[...] = jnp.zeros(l_ref.shape, F32)
        accd_ref[...] = jnp.zeros(accd_ref.shape, F32)
        c_ref[...] = jnp.zeros(c_ref.shape, F32)
        accs_ref[...] = jnp.zeros(accs_ref.shape, F32)
        pad_ref[SUB:PAGE, :] = jnp.zeros((PAGE - SUB, GROUP_W), F32)
        tok_d = lax.broadcasted_iota(jnp.int32, (nd, PAGE), 0) % SUB
        key_d = lax.broadcasted_iota(jnp.int32, (nd, PAGE), 1)
        tok_s = lax.broadcasted_iota(jnp.int32, (ns, PAGE), 0) % SUB
        key_s = lax.broadcasted_iota(jnp.int32, (ns, PAGE), 1)

        def page_of(ref):
            pad_ref[0:SUB, :] = ref[q]
            return pad_ref[...].T.astype(BF16)

        k_new = page_of(dk_ref)
        v_new = page_of(dv_ref)
        sk_new = page_of(sk_ref)
        sv_new = page_of(sv_ref)
        attend([k_new], [v_new], [sk_new], [sv_new],
               (key_d <= tok_d) & (key_d < n_new), (key_s < tok_s) & (key_s < n_new))

    def finish(q):
        lam = _lam(*lam_refs, lam_init)
        lane_head = lax.broadcasted_iota(jnp.int32, (SUB, GROUP_W), 1) // HEAD_DIM
        l_tot = jnp.sum(l_ref[...], axis=1, keepdims=True)
        yd = jnp.zeros((SUB, GROUP_W), F32)
        ys = jnp.zeros((SUB, GROUP_W), F32)
        for h in range(N_HEADS):
            r1 = h * SUB
            r2 = (N_HEADS + h) * SUB
            o = (accd_ref[r1:r1 + SUB, :] / l_tot[r1:r1 + SUB, :]
                 - lam * (accd_ref[r2:r2 + SUB, :] / l_tot[r2:r2 + SUB, :]))
            sel = lane_head == h
            ms = jnp.sum(jnp.where(sel, o * o, 0.0), axis=-1, keepdims=True) / HEAD_DIM
            yd = jnp.where(sel, o * lax.rsqrt(ms + EPS), yd)
            ys = jnp.where(sel, accs_ref[r1:r1 + SUB, :], ys)
        yd_ref[q] = yd * g_ref[...] * (1.0 - lam_init)
        ys_ref[q] = ys

    return begin, attend, finish


def _tail_units(x_ref, yp_ref, yd_ref, ys_ref, yb_ref, p_ref, wout_ref, wup_ref, wdown_ref, wgate_ref, wpp_ref,
                g_mix, g_pre, g_post, g_ple, o_ref, *, ff_chunk, attn_transposed):
    st = {}

    def mix():
        yd = yd_ref[0].T if attn_transposed else yd_ref[...]
        yb = yb_ref[0].T if attn_transposed else yb_ref[...]
        m = jnp.concatenate([yp_ref[...], yd, ys_ref[...], yb], axis=-1).astype(BF16)
        st["x"] = x_ref[...] + _rms(jnp.dot(m, wout_ref[...], preferred_element_type=F32), g_mix[...])
        st["h"] = _rms(st["x"], g_pre[...]).astype(BF16)

    def up(c):
        u = jnp.dot(st["h"], wup_ref[:, c * ff_chunk:(c + 1) * ff_chunk], preferred_element_type=F32)
        st["u"] = jnp.square(jnp.maximum(u, 0.0)).astype(BF16)

    def down(c):
        d = jnp.dot(st["u"], wdown_ref[c * ff_chunk:(c + 1) * ff_chunk, :], preferred_element_type=F32)
        st["f"] = d if c == 0 else st["f"] + d

    def gate():
        st["x"] = st["x"] + _rms(st["f"], g_post[...])
        st["gate"] = jax.nn.sigmoid(jnp.dot(st["x"].astype(BF16), wgate_ref[...], preferred_element_type=F32))

    def out():
        pp = jnp.dot(p_ref[...].astype(BF16), wpp_ref[...], preferred_element_type=F32)
        o_ref[...] = st["x"] + _rms(st["gate"] * pp, g_ple[...])

    units = [mix]
    for c in range(wup_ref.shape[1] // ff_chunk):
        units += [functools.partial(up, c), functools.partial(down, c)]
    return units + [gate, out]


N_TAIL_IN = 15
N_CACHES = 4


def _tail_kernel(*refs, ff_chunk, attn_transposed):
    for unit in _tail_units(*refs, ff_chunk=ff_chunk, attn_transposed=attn_transposed):
        unit()


def _tail_decode_kernel(pt_ref, *refs, ff_chunk, layer, pages_per_step, n_pages, n_seq, seq_per_tile, n_new,
                        lam_init):
    G = pages_per_step
    tail_in = refs[:N_TAIL_IN]
    tok_refs = refs[N_TAIL_IN:N_TAIL_IN + 6]
    lam_refs = refs[N_TAIL_IN + 6:N_TAIL_IN + 10]
    g_ref = refs[N_TAIL_IN + 10]
    caches = refs[N_TAIL_IN + 11:N_TAIL_IN + 11 + N_CACHES]
    o_ref, yd_ref, ys_ref = refs[N_TAIL_IN + 11 + N_CACHES:N_TAIL_IN + 14 + N_CACHES]
    scratch = refs[N_TAIL_IN + 14 + N_CACHES:]
    state, buf_ref, sem_ref = scratch[:8], scratch[8], scratch[9]
    begin, attend, finish = _decode_fns(tok_refs, lam_refs, g_ref, yd_ref, ys_ref, state,
                                        n_new=n_new, lam_init=lam_init)
    units = _tail_units(*tail_in, o_ref, ff_chunk=ff_chunk, attn_transposed=True)
    t = pl.program_id(0)
    steps = n_pages // G
    n_sub = seq_per_tile * steps

    def copies(seq, k, slot):
        out = []
        for g in range(G):
            page = pt_ref[seq, n_pages - 1 - (k * G + g)]
            for c in range(N_CACHES):
                out.append(pltpu.make_async_copy(caches[c].at[layer, page], buf_ref.at[slot, c, g], sem_ref.at[slot]))
        return out

    @pl.when(t == 0)
    def _():
        for cp in copies(0, 0, 0):
            cp.start()

    for j in range(n_sub):
        q, k = divmod(j, steps)
        slot = j % 2
        seq = t * seq_per_tile + q
        for cp in copies(seq, k, slot):
            cp.wait()
        q1, k1 = divmod(j + 1, steps) if j + 1 < n_sub else (seq_per_tile, 0)
        for cp in copies(jnp.minimum(t * seq_per_tile + q1, n_seq - 1), k1, 1 - slot):
            cp.start()
        if k == 0:
            begin(q)
        attend(*[[buf_ref[slot, c, g].astype(BF16) for g in range(G)] for c in range(N_CACHES)], None, None)
        if k == steps - 1:
            finish(q)
        for unit in units[j * len(units) // n_sub:(j + 1) * len(units) // n_sub]:
            unit()

    @pl.when(t == pl.num_programs(0) - 1)
    def _():
        for cp in copies(n_seq - 1, 0, 0):
            cp.wait()


def _tail_specs(tm, D, ple_dim, weights, attn, imap):
    def resident(shape):
        return pl.BlockSpec(shape, imap(lambda i: (0,) * len(shape)), pipeline_mode=pl.Buffered(1))

    row = lambda w: pl.BlockSpec((tm, w), imap(lambda i: (i, 0)))
    specs = [row(D), row(GROUP_W), attn, row(GROUP_W), attn, row(ple_dim)]
    return specs + [resident(w.shape) for w in weights] + [resident((1, D))] * 4, row(D)


def _tail(x2d, ys, p2d, weights, gains, tm):
    T, D = x2d.shape
    imap = lambda f: f
    in_specs, out_spec = _tail_specs(tm, D, p2d.shape[1], weights, pl.BlockSpec((tm, GROUP_W), lambda i: (i, 0)), imap)
    return pl.pallas_call(
        functools.partial(_tail_kernel, ff_chunk=min(weights[1].shape[1], 1024), attn_transposed=False),
        grid=(T // tm,),
        in_specs=in_specs,
        out_specs=out_spec,
        out_shape=jax.ShapeDtypeStruct((T, D), F32),
        compiler_params=pltpu.CompilerParams(dimension_semantics=("arbitrary",), vmem_limit_bytes=VMEM_LIMIT),
        name="tail",
    )(x2d, *ys, p2d, *weights, *gains)


def _tail_decode(x2d, ys, p2d, weights, gains, tm, seq, layer, page_table, new_rows, caches_t, lams, subln_tiled,
                 lam_init, n_new, pages_per_step):
    T, D = x2d.shape
    n_tiles = T // tm
    n_seq, n_pages = page_table.shape
    G = pages_per_step
    assert n_seq % n_tiles == 0 and n_pages % G == 0
    spt = n_seq // n_tiles
    assert (spt * (n_pages // G)) % 2 == 0
    tps = seq // tm
    imap = lambda f: (lambda i, pt: f(i))
    attn = pl.BlockSpec((1, GROUP_W, tm), lambda i, pt: (i // tps, 0, i % tps))
    tail_specs, out_spec = _tail_specs(tm, D, p2d.shape[1], weights, attn, imap)
    tok_spec = pl.BlockSpec((spt, SUB, GROUP_W), lambda i, pt: (i, 0, 0))
    vec = pl.BlockSpec((1, DIFF_QK), lambda i, pt: (0, 0))
    nd = 2 * N_HEADS * SUB
    ns = N_HEADS * SUB
    tok_shape = jax.ShapeDtypeStruct((n_seq, SUB, GROUP_W), F32)
    return pl.pallas_call(
        functools.partial(_tail_decode_kernel, ff_chunk=min(weights[1].shape[1], 1024), layer=layer,
                          pages_per_step=G, n_pages=n_pages, n_seq=n_seq, seq_per_tile=spt, n_new=n_new,
                          lam_init=lam_init),
        grid_spec=pltpu.PrefetchScalarGridSpec(
            num_scalar_prefetch=1,
            grid=(n_tiles,),
            in_specs=(tail_specs + [tok_spec] * 6 + [vec] * 4 + [pl.BlockSpec((1, GROUP_W), lambda i, pt: (0, 0))]
                      + [pl.BlockSpec(memory_space=pl.ANY)] * N_CACHES),
            out_specs=[out_spec, tok_spec, tok_spec],
            scratch_shapes=[pltpu.VMEM((nd, GROUP_W), BF16), pltpu.VMEM((ns, GROUP_W), BF16),
                            pltpu.VMEM((PAGE, GROUP_W), F32),
                            pltpu.VMEM((nd, 1), F32), pltpu.VMEM((nd, PAGE), F32), pltpu.VMEM((nd, GROUP_W), F32),
                            pltpu.VMEM((ns, PAGE), F32), pltpu.VMEM((ns, GROUP_W), F32),
                            pltpu.VMEM((2, N_CACHES, G, GROUP_W, PAGE), F32),
                            pltpu.SemaphoreType.DMA((2,))]),
        out_shape=[jax.ShapeDtypeStruct((T, D), F32), tok_shape, tok_shape],
        compiler_params=pltpu.CompilerParams(dimension_semantics=("arbitrary",), vmem_limit_bytes=VMEM_LIMIT),
        name="tail_decode",
    )(page_table, x2d, *ys, p2d, *weights, *gains, *new_rows, *lams, subln_tiled, *caches_t)


def _block_diag(w):
    n, c, _ = w.shape
    eye = jnp.eye(n, dtype=w.dtype)
    return (eye[:, None, :, None] * w[:, :, None, :]).reshape(n * c, n * c)


def _channel_major_pages(cache):
    nd = cache.ndim
    perm = (0, 1) + tuple(range(3, nd)) + (2,)
    return jnp.transpose(cache, perm).reshape(cache.shape[0], cache.shape[1], GROUP_W, PAGE)


def _rows_from_channel_major(t, tail_shape):
    b, _, s = t.shape
    n = len(tail_shape)
    return jnp.transpose(t.reshape((b,) + tail_shape + (s,)), (0, n + 1) + tuple(range(1, n + 1)))


def kernel(x_prompt, x_sample, state_pool, cache_diff_k, cache_diff_v, cache_sb_k, cache_sb_v, page_table,
           p_prompt, p_sample, w_in, w_out, norm_mix_pre, norm_mix_post, norm_ffn_pre, norm_ffn_post, norm_ple,
           pool_w, pool_scale, diff_lam_q1, diff_lam_k1, diff_lam_q2, diff_lam_k2, diff_subln,
           sgu_w, sgu_b, w_ffn_up, w_ffn_down, w_ple_gate, w_ple_proj):
    batch, seq, d_model = x_prompt.shape
    nb, dec_seq, _ = x_sample.shape
    depth = w_in.shape[0]
    n_pages = page_table.shape[1]
    past_len = n_pages * PAGE
    ple_dim = p_prompt.shape[-1]
    assert d_model == 4 * GROUP_W and dec_seq <= SUB and seq % CHUNK == 0 and (nb * SUB) % CHUNK == 0

    tm = min(512, seq)
    tq = min(256, seq)
    ts = nb * SUB
    pages_per_step = 8 if n_pages % 8 == 0 else 1

    tables_p = _rope_tables(jnp.arange(seq))
    tables_s = _rope_tables(jnp.tile(past_len + jnp.arange(SUB), nb))

    xp = x_prompt.reshape(batch * seq, d_model)
    pad_s = lambda t: jnp.pad(t, ((0, 0), (0, SUB - dec_seq), (0, 0))).reshape(ts, -1)
    xs = pad_s(x_sample)
    caches_t = [_channel_major_pages(c) for c in (cache_diff_k, cache_diff_v, cache_sb_k, cache_sb_v)]

    outs = {name: [] for name in ("pool_p", "pool_s", "dk_p", "dv_p", "sk_p", "sv_p",
                                  "dk_s", "dv_s", "sk_s", "sv_s", "sgu_s")}
    unpad = lambda t: t.reshape(nb, SUB, GROUP_W)[:, :dec_seq]
    qk_shape = (N_HEADS, 2, DIFF_QK)
    v_shape = (N_HEADS, HEAD_DIM)

    for i in range(depth):
        lam_init = 0.8 - 0.6 * math.exp(-0.3 * i)
        w_in_b = w_in[i].astype(BF16)
        weights = tuple(w[i].astype(BF16) for w in (w_out, w_ffn_up, w_ffn_down, w_ple_gate, w_ple_proj))
        gains = tuple(g[i].reshape(1, d_model) for g in (norm_mix_post, norm_ffn_pre, norm_ffn_post, norm_ple))
        g_pre = norm_mix_pre[i].reshape(1, d_model)
        pw_bd = _block_diag(pool_w[i]).astype(BF16)
        pscale = pool_scale[i].reshape(1, GROUP_W)
        lams = tuple(v[i].reshape(1, DIFF_QK) for v in (diff_lam_q1, diff_lam_k1, diff_lam_q2, diff_lam_k2))
        subln = diff_subln[i].reshape(1, HEAD_DIM)
        sgu_bias_p = jnp.repeat(sgu_b[i].T, HEAD_DIM, axis=1)
        reps = CHUNK // SUB
        sgu_w_s = jax.vmap(lambda w: jnp.kron(jnp.eye(reps, dtype=w.dtype), w[:SUB, :SUB]))(sgu_w[i])
        sgu_bias_s = jnp.tile(jnp.repeat(sgu_b[i][:, :SUB].T, HEAD_DIM, axis=1), (reps, 1))

        a_s, dq, dk_s, dv_s, su, sv, sq, sk_s, sv_s = _proj(xs, g_pre, w_in_b, tables_s, ts)
        halo = jnp.concatenate([jnp.zeros((nb, HALO - POOL_BUF, GROUP_W), F32), state_pool[i]], axis=1)
        y_pool_s = _pool_sample(a_s.reshape(nb, SUB, GROUP_W), halo, pw_bd, pscale, past_len).reshape(ts, GROUP_W)
        y_sgu_s, sgu_v = _sgu(su, sv, sgu_w_s, sgu_bias_s, CHUNK)
        new_rows = tuple(t.reshape(nb, SUB, GROUP_W) for t in (dq, dk_s, dv_s, sq, sk_s, sv_s))

        a, dqt, dk, dkt, dvt, su, sv, sqt, sk, skt, svt = _proj(xp, g_pre, w_in_b, tables_p, tm, seq=seq)
        y_pool = _pool_prompt(a, pw_bd, pscale, seq, tm)
        y_sgu, _ = _sgu(su, sv, sgu_w[i], sgu_bias_p, tm)
        y_diff_t = _diff_prompt(dqt, dk.reshape(batch, seq, GROUP_W), dvt, lams, subln.reshape(HEAD_DIM, 1),
                                lam_init, tq)
        y_sb_t = _sb_prompt(sqt, sk.reshape(batch, seq, GROUP_W), svt, tq)
        xp, y_diff_s, y_sb_s = _tail_decode(
            xp, (y_pool, y_diff_t, y_sgu, y_sb_t), p_prompt[i].reshape(batch * seq, ple_dim), weights, gains, tm, seq,
            i, page_table, new_rows, caches_t, lams, jnp.tile(subln, (1, N_HEADS)), lam_init, dec_seq, pages_per_step)
        outs["pool_p"].append(a.reshape(batch, seq, GROUP_W)[:, seq - POOL_BUF:])
        outs["dk_p"].append(_rows_from_channel_major(dkt, qk_shape))
        outs["dv_p"].append(_rows_from_channel_major(dvt, v_shape))
        outs["sk_p"].append(_rows_from_channel_major(skt, v_shape))
        outs["sv_p"].append(_rows_from_channel_major(svt, v_shape))

        xs = _tail(xs, (y_pool_s, y_diff_s.reshape(ts, GROUP_W), y_sgu_s, y_sb_s.reshape(ts, GROUP_W)),
                   pad_s(p_sample[i]), weights, gains, ts)
        outs["pool_s"].append(jnp.concatenate([state_pool[i], unpad(a_s)], axis=1)[:, -POOL_BUF:])
        outs["dk_s"].append(unpad(dk_s).reshape((nb, dec_seq) + qk_shape))
        outs["dv_s"].append(unpad(dv_s).reshape((nb, dec_seq) + v_shape))
        outs["sk_s"].append(unpad(sk_s).reshape((nb, dec_seq) + v_shape))
        outs["sv_s"].append(unpad(sv_s).reshape((nb, dec_seq) + v_shape))
        outs["sgu_s"].append(unpad(sgu_v))

    st = lambda name: jnp.stack(outs[name])
    return (xp.reshape(batch, seq, d_model), xs.reshape(nb, SUB, d_model)[:, :dec_seq],
            st("pool_p"), st("pool_s"),
            st("dk_p"), st("dv_p"), st("sk_p"), st("sv_p"),
            st("dk_s"), st("dv_s"), st("sk_s"), st("sv_s"),
            st("sgu_s"))
```
